```python
import math
import jax, jax.numpy as jnp
from jax import lax
import numpy as np

D_MODEL = 2048
BATCH = 2
SEQ = 16384
DEPTH = 2

N_SSM_LAYERS = DEPTH // 2
N_MLA_LAYERS = DEPTH - N_SSM_LAYERS
MEM_TOKENS = 256
MEM_HEADS = 4
MEM_WIDTH = D_MODEL // 4
MEM_HEAD_DIM = MEM_WIDTH // MEM_HEADS
MAIN_WIDTH = D_MODEL - MEM_WIDTH
SSM_GROUP = 16
SSM_GROUPS = MAIN_WIDTH // SSM_GROUP
SSM_STATE = 64
SSM_CHUNK = 256
DT_MIN, DT_MAX = 0.001, 0.1
MLA_V_DIM = 128
MLA_HEADS = MAIN_WIDTH // MLA_V_DIM
MLA_NOPE = 128
MLA_ROPE = 64
MLA_QK = MLA_NOPE + MLA_ROPE
Q_LORA = D_MODEL // 4
KV_LORA = D_MODEL // 4
ROPE_BASE = 10000.0
ATTN_BLOCK = 128
ALPHA = (2.0 * DEPTH) ** 0.25
BETA = (8.0 * DEPTH) ** -0.25
LN_EPS = 1e-5
RMS_EPS = 1e-6
NEG_INF = -1e30

W_IN_A_COLS = 2 * MAIN_WIDTH + 2 * MEM_WIDTH
W_IN_B_COLS = Q_LORA + MAIN_WIDTH + 2 * MEM_WIDTH

kernel_name = "yoco_s5_mla_memory_deepnorm"


def layer_norm(x, g, b):
    xf = x.astype(jnp.float32)
    mu = jnp.mean(xf, axis=-1, keepdims=True)
    var = jnp.mean(jnp.square(xf - mu), axis=-1, keepdims=True)
    out = (xf - mu) * lax.rsqrt(var + LN_EPS) * g.astype(jnp.float32) + b.astype(jnp.float32)
    return out.astype(x.dtype)


def rms_norm(x, g):
    xf = x.astype(jnp.float32)
    out = xf * lax.rsqrt(jnp.mean(jnp.square(xf), axis=-1, keepdims=True) + RMS_EPS) * g.astype(jnp.float32)
    return out.astype(x.dtype)


def rope_tables(positions):
    inv_freq = ROPE_BASE ** (-jnp.arange(0, MLA_ROPE, 2, dtype=jnp.float32) / MLA_ROPE)
    ang = positions.astype(jnp.float32)[..., None] * inv_freq
    return jnp.cos(ang), jnp.sin(ang)


def apply_rope(x, cos, sin):
    xf = x.astype(jnp.float32)
    x1, x2 = jnp.split(xf, 2, axis=-1)
    out = jnp.concatenate([x1 * cos - x2 * sin, x2 * cos + x1 * sin], axis=-1)
    return out.astype(x.dtype)


def _complex_scan_combine(left, right):
    a1r, a1i, b1r, b1i = left
    a2r, a2i, b2r, b2i = right
    return (a2r * a1r - a2i * a1i,
            a2r * a1i + a2i * a1r,
            a2r * b1r - a2i * b1i + b2r,
            a2r * b1i + a2i * b1r + b2i)


def s5_branch(u, lam_re, lam_im, log_dt, b_re, b_im, c_re, c_im, d_skip, w_glu, b_glu):
    bsz, seq, _ = u.shape
    uf = u.astype(jnp.float32).reshape(bsz, seq, SSM_GROUPS, SSM_GROUP)
    lr = lam_re.astype(jnp.float32)
    li = lam_im.astype(jnp.float32)
    dt = jnp.exp(log_dt.astype(jnp.float32))[:, None]
    mag = jnp.exp(lr * dt)
    a_re = mag * jnp.cos(li * dt)
    a_im = mag * jnp.sin(li * dt)
    den = lr * lr + li * li
    n_re, n_im = a_re - 1.0, a_im
    coef_re = (n_re * lr + n_im * li) / den
    coef_im = (n_im * lr - n_re * li) / den
    br, bi = b_re.astype(jnp.float32), b_im.astype(jnp.float32)
    bb_re = coef_re[..., None] * br - coef_im[..., None] * bi
    bb_im = coef_re[..., None] * bi + coef_im[..., None] * br
    cr, ci = c_re.astype(jnp.float32), c_im.astype(jnp.float32)
    dsk = d_skip.astype(jnp.float32)

    chunk = math.gcd(seq, SSM_CHUNK)
    n_chunks = seq // chunk
    uc = uf.reshape(bsz, n_chunks, chunk, SSM_GROUPS, SSM_GROUP).transpose(1, 0, 2, 3, 4)

    def step(carry, u_blk):
        h_re, h_im = carry
        bu_re = jnp.einsum('blgh,gph->blgp', u_blk, bb_re)
        bu_im = jnp.einsum('blgh,gph->blgp', u_blk, bb_im)
        ar_b = jnp.broadcast_to(a_re, bu_re.shape)
        ai_b = jnp.broadcast_to(a_im, bu_re.shape)
        pa_re, pa_im, s_re, s_im = lax.associative_scan(
            _complex_scan_combine, (ar_b, ai_b, bu_re, bu_im), axis=1)
        x_re = s_re + pa_re * h_re[:, None] - pa_im * h_im[:, None]
        x_im = s_im + pa_re * h_im[:, None] + pa_im * h_re[:, None]
        y = (jnp.einsum('blgp,ghp->blgh', x_re, cr)
             - jnp.einsum('blgp,ghp->blgh', x_im, ci)
             + dsk * u_blk)
        return (x_re[:, -1], x_im[:, -1]), y

    init = (jnp.zeros((bsz, SSM_GROUPS, SSM_STATE), jnp.float32),
            jnp.zeros((bsz, SSM_GROUPS, SSM_STATE), jnp.float32))
    _, ys = lax.scan(step, init, uc)
    y = ys.transpose(1, 0, 2, 3, 4).reshape(bsz, seq, MAIN_WIDTH).astype(u.dtype)
    g = jax.nn.gelu(y)
    return g * jax.nn.sigmoid(g @ w_glu + b_glu)


def causal_block_attention(q, k, v):
    bsz, seq, nh, dk = q.shape
    dv = v.shape[-1]
    nb = seq // ATTN_BLOCK
    scale = dk ** -0.5
    qb = q.reshape(bsz, nb, ATTN_BLOCK, nh, dk).transpose(1, 0, 2, 3, 4)
    offs = jnp.arange(ATTN_BLOCK)

    def one_block(args):
        qi, i = args
        qpos = i * ATTN_BLOCK + offs

        def body(j, carry):
            m, l, acc = carry
            kj = lax.dynamic_slice_in_dim(k, j * ATTN_BLOCK, ATTN_BLOCK, axis=1)
            vj = lax.dynamic_slice_in_dim(v, j * ATTN_BLOCK, ATTN_BLOCK, axis=1)
            s = jnp.einsum('bqhd,bkhd->bhqk', qi, kj,
                           preferred_element_type=jnp.float32) * scale
            kpos = j * ATTN_BLOCK + offs
            s = jnp.where(kpos[None, :] <= qpos[:, None], s, NEG_INF)
            m_new = jnp.maximum(m, jnp.max(s, axis=-1))
            p = jnp.exp(s - m_new[..., None])
            corr = jnp.exp(m - m_new)
            l_new = l * corr + jnp.sum(p, axis=-1)
            acc_new = acc * corr[..., None] + jnp.einsum(
                'bhqk,bkhd->bhqd', p, vj.astype(jnp.float32))
            return m_new, l_new, acc_new

        init = (jnp.full((bsz, nh, ATTN_BLOCK), NEG_INF, jnp.float32),
                jnp.zeros((bsz, nh, ATTN_BLOCK), jnp.float32),
                jnp.zeros((bsz, nh, ATTN_BLOCK, dv), jnp.float32))
        m, l, acc = lax.fori_loop(0, i + 1, body, init)
        return (acc / l[..., None]).transpose(0, 2, 1, 3)

    out = lax.map(one_block, (qb, jnp.arange(nb)))
    return out.transpose(1, 0, 2, 3, 4).reshape(bsz, seq, nh, dv).astype(q.dtype)


def memory_attention(mq, mem, w_mem_kv):
    bsz, seq, _ = mq.shape
    kv = mem @ w_mem_kv
    mk, mv = jnp.split(kv, 2, axis=-1)
    mk = mk.reshape(bsz, -1, MEM_HEADS, MEM_HEAD_DIM)
    mv = mv.reshape(bsz, -1, MEM_HEADS, MEM_HEAD_DIM)
    q = mq.reshape(bsz, seq, MEM_HEADS, MEM_HEAD_DIM)
    s = jnp.einsum('bshd,bmhd->bhsm', q, mk,
                   preferred_element_type=jnp.float32) * (MEM_HEAD_DIM ** -0.5)
    p = jax.nn.softmax(s, axis=-1)
    o = jnp.einsum('bhsm,bmhd->bshd', p, mv.astype(jnp.float32))
    return o.reshape(bsz, seq, MEM_WIDTH).astype(mq.dtype)


def setup_inputs(seed: int = 0) -> dict:
    key = jax.random.key(seed)
    ks = jax.random.split(key, 32)
    f32 = jnp.float32
    nrm = lambda k, shape, std: jax.random.normal(k, shape, f32) * std
    n_idx = jnp.arange(SSM_STATE, dtype=f32)
    lam_re = -0.5 + nrm(ks[3], (N_SSM_LAYERS, SSM_GROUPS, SSM_STATE), 0.01)
    lam_im = math.pi * n_idx + nrm(ks[4], (N_SSM_LAYERS, SSM_GROUPS, SSM_STATE), 0.01)
    log_dt = jax.random.uniform(ks[5], (N_SSM_LAYERS, SSM_GROUPS), f32,
                                math.log(DT_MIN), math.log(DT_MAX))
    return {
        "x": nrm(ks[0], (BATCH, SEQ, D_MODEL), 1.0),
        "mem": nrm(ks[1], (BATCH, MEM_TOKENS, D_MODEL), 1.0),
        "positions": jnp.broadcast_to(jnp.arange(SEQ, dtype=jnp.int32), (BATCH, SEQ)),
        "w_in_a": nrm(ks[2], (N_SSM_LAYERS, D_MODEL, W_IN_A_COLS), D_MODEL ** -0.5),
        "lam_re": lam_re,
        "lam_im": lam_im,
        "log_dt": log_dt,
        "b_re": nrm(ks[6], (N_SSM_LAYERS, SSM_GROUPS, SSM_STATE, SSM_GROUP), (2 * SSM_GROUP) ** -0.5),
        "b_im": nrm(ks[7], (N_SSM_LAYERS, SSM_GROUPS, SSM_STATE, SSM_GROUP), (2 * SSM_GROUP) ** -0.5),
        "c_re": nrm(ks[8], (N_SSM_LAYERS, SSM_GROUPS, SSM_GROUP, SSM_STATE), (2 * SSM_STATE) ** -0.5),
        "c_im": nrm(ks[9], (N_SSM_LAYERS, SSM_GROUPS, SSM_GROUP, SSM_STATE), (2 * SSM_STATE) ** -0.5),
        "d_skip": nrm(ks[10], (N_SSM_LAYERS, SSM_GROUPS, SSM_GROUP), 1.0),
        "w_glu": nrm(ks[11], (N_SSM_LAYERS, MAIN_WIDTH, MAIN_WIDTH), MAIN_WIDTH ** -0.5),
        "b_glu": nrm(ks[12], (N_SSM_LAYERS, MAIN_WIDTH), 0.01),
        "w_in_b": nrm(ks[13], (N_MLA_LAYERS, D_MODEL, W_IN_B_COLS), D_MODEL ** -0.5),
        "q_norm_g": 1.0 + nrm(ks[14], (N_MLA_LAYERS, Q_LORA), 0.02),
        "w_uq": nrm(ks[15], (N_MLA_LAYERS, Q_LORA, MLA_HEADS, MLA_QK), Q_LORA ** -0.5),
        "w_dkv": nrm(ks[16], (D_MODEL, KV_LORA), D_MODEL ** -0.5),
        "kv_norm_g": 1.0 + nrm(ks[17], (KV_LORA,), 0.02),
        "w_kr": nrm(ks[18], (D_MODEL, MLA_ROPE), D_MODEL ** -0.5),
        "w_uk": nrm(ks[19], (KV_LORA, MLA_HEADS, MLA_NOPE), KV_LORA ** -0.5),
        "w_uv": nrm(ks[20], (KV_LORA, MLA_HEADS, MLA_V_DIM), KV_LORA ** -0.5),
        "w_mem_kv": nrm(ks[21], (DEPTH, D_MODEL, 2 * MEM_WIDTH), D_MODEL ** -0.5),
        "w_out": nrm(ks[22], (DEPTH, D_MODEL, D_MODEL), BETA * D_MODEL ** -0.5),
        "ln_g": 1.0 + nrm(ks[23], (DEPTH, D_MODEL), 0.02),
        "ln_b": nrm(ks[24], (DEPTH, D_MODEL), 0.02),
    }


def reference(x, mem, positions, w_in_a, lam_re, lam_im, log_dt, b_re, b_im, c_re, c_im,
              d_skip, w_glu, b_glu, w_in_b, q_norm_g, w_uq, w_dkv, kv_norm_g, w_kr,
              w_uk, w_uv, w_mem_kv, w_out, ln_g, ln_b):
    bsz, seq, _ = x.shape
    cos, sin = rope_tables(positions)
    k_shared = None
    v_shared = None
    for layer in range(DEPTH):
        if layer < N_SSM_LAYERS:
            i = layer
            h = x @ w_in_a[i]
            u, z, mq, mz = jnp.split(
                h, [MAIN_WIDTH, 2 * MAIN_WIDTH, 2 * MAIN_WIDTH + MEM_WIDTH], axis=-1)
            main = s5_branch(u, lam_re[i], lam_im[i], log_dt[i], b_re[i], b_im[i],
                             c_re[i], c_im[i], d_skip[i], w_glu[i], b_glu[i])
        else:
            j = layer - N_SSM_LAYERS
            h = x @ w_in_b[j]
            cq, z, mq, mz = jnp.split(
                h, [Q_LORA, Q_LORA + MAIN_WIDTH, Q_LORA + MAIN_WIDTH + MEM_WIDTH], axis=-1)
            q = jnp.einsum('bsc,chd->bshd', rms_norm(cq, q_norm_g[j]), w_uq[j])
            q_nope, q_rope = jnp.split(q, [MLA_NOPE], axis=-1)
            q_rope = apply_rope(q_rope, cos[:, :, None, :], sin[:, :, None, :])
            q = jnp.concatenate([q_nope, q_rope], axis=-1)
            attn = causal_block_attention(q, k_shared, v_shared)
            main = attn.reshape(bsz, seq, MAIN_WIDTH)
        main = main * jax.nn.silu(z)
        memo = memory_attention(mq, mem, w_mem_kv[layer]) * jax.nn.silu(mz)
        y = jnp.concatenate([main, memo], axis=-1) @ w_out[layer]
        x = layer_norm(ALPHA * x + y, ln_g[layer], ln_b[layer])
        if layer == N_SSM_LAYERS - 1:
            c_kv = rms_norm(x @ w_dkv, kv_norm_g)
            k_rope = apply_rope(x @ w_kr, cos, sin)
            k_nope = jnp.einsum('bsc,chd->bshd', c_kv, w_uk)
            v_shared = jnp.einsum('bsc,chd->bshd', c_kv, w_uv)
            k_shared = jnp.concatenate(
                [k_nope, jnp.broadcast_to(k_rope[:, :, None, :], (bsz, seq, MLA_HEADS, MLA_ROPE))],
                axis=-1)
    return x
```

```python
import functools
import math

import numpy as np
import jax
import jax.numpy as jnp
from jax import lax
from jax.experimental import pallas as pl
from jax.experimental.pallas import tpu as pltpu

F32 = jnp.float32
BF16 = jnp.bfloat16

MEM_HEADS = 4
MEM_HEAD_DIM = 128
SSM_GROUP = 16
SSM_STATE = 64
MLA_V_DIM = 128
MLA_NOPE = 128
MLA_ROPE = 64
MLA_QK = MLA_NOPE + MLA_ROPE
ROPE_BASE = 10000.0
LN_EPS = 1e-5
RMS_EPS = 1e-6
NEG_INF = -1e30

LANES = 128
SUBLANES = 8
VMEM_LIMIT = 52 * 1024 * 1024
S5_CHUNK_GROUPS = 16
S5_LANE_SPLIT = 512
S5_TIME_TILE = 512
FLASH_TQ = 1024
FLASH_TK = 1024


def _cparams(sem):
    return pltpu.CompilerParams(dimension_semantics=sem, vmem_limit_bytes=VMEM_LIMIT)


def _dot(a, b):
    return jnp.dot(a, b, preferred_element_type=F32)


def _sigmoid(x):
    return 1.0 / (1.0 + jnp.exp(-x))


def _rope_table_kernel(pos_ref, invf_ref, sign_ref, cos_ref, sin_ref):
    ang = pos_ref[...].astype(F32) * invf_ref[...]
    cos_ref[...] = jnp.cos(ang)
    sin_ref[...] = jnp.sin(ang) * sign_ref[...]


def _rope_tables(positions):
    bsz, seq = positions.shape
    tokens = bsz * seq
    bm = min(2048, tokens)
    inv_freq = ROPE_BASE ** (-jnp.arange(0, MLA_ROPE, 2, dtype=F32) / MLA_ROPE)
    invf = jnp.tile(inv_freq, 4).reshape(1, LANES)
    sign = jnp.tile(jnp.concatenate([-jnp.ones((MLA_ROPE // 2,), F32),
                                     jnp.ones((MLA_ROPE // 2,), F32)]), 2).reshape(1, LANES)
    cos, sin = pl.pallas_call(
        _rope_table_kernel,
        out_shape=(jax.ShapeDtypeStruct((tokens, LANES), F32),
                   jax.ShapeDtypeStruct((tokens, LANES), F32)),
        grid=(tokens // bm,),
        in_specs=[pl.BlockSpec((bm, 1), lambda i: (i, 0)),
                  pl.BlockSpec((1, LANES), lambda i: (0, 0)),
                  pl.BlockSpec((1, LANES), lambda i: (0, 0))],
        out_specs=(pl.BlockSpec((bm, LANES), lambda i: (i, 0)),
                   pl.BlockSpec((bm, LANES), lambda i: (i, 0))),
        compiler_params=_cparams(("parallel",)),
        name="rope_tables",
    )(positions.reshape(tokens, 1), invf, sign)
    return cos.reshape(bsz, seq, LANES), sin.reshape(bsz, seq, LANES)


def _inproj_kernel(x_ref, w_ref, o_ref, xb_ref):
    @pl.when(pl.program_id(1) == 0)
    def _():
        xb_ref[...] = x_ref[...].astype(BF16)

    o_ref[...] = _dot(xb_ref[...], w_ref[...]).astype(o_ref.dtype)


def _inproj(x2d, w_bf16, name):
    m, k = x2d.shape
    n = w_bf16.shape[1]
    bm = min(512, m)
    bn = min(1024, n)
    return pl.pallas_call(
        _inproj_kernel,
        out_shape=jax.ShapeDtypeStruct((m, n), BF16),
        grid=(m // bm, n // bn),
        in_specs=[pl.BlockSpec((bm, k), lambda i, j: (i, 0)),
                  pl.BlockSpec((k, bn), lambda i, j: (0, j))],
        out_specs=pl.BlockSpec((bm, bn), lambda i, j: (i, j)),
        scratch_shapes=[pltpu.VMEM((bm, k), BF16)],
        compiler_params=_cparams(("parallel", "arbitrary")),
        name=name,
    )(x2d, w_bf16)


def _s5_param_kernel(lr_ref, li_ref, ldt_ref, br_ref, bi_ref,
                     are_ref, aim_ref, alre_ref, alim_ref, bbr_ref, bbi_ref, *, seg_len):
    lr = lr_ref[...]
    li = li_ref[...]
    dt = jnp.exp(ldt_ref[...])
    mag = jnp.exp(lr * dt)
    a_re = mag * jnp.cos(li * dt)
    a_im = mag * jnp.sin(li * dt)
    den = lr * lr + li * li
    n_re = a_re - 1.0
    n_im = a_im
    coef_re = (n_re * lr + n_im * li) / den
    coef_im = (n_im * lr - n_re * li) / den
    br = br_ref[...]
    bi = bi_ref[...]
    bbr_ref[...] = coef_re * br - coef_im * bi
    bbi_ref[...] = coef_re * bi + coef_im * br
    are_ref[...] = a_re
    aim_ref[...] = a_im
    mag_l = jnp.exp(lr * dt * float(seg_len))
    alre_ref[...] = mag_l * jnp.cos(li * dt * float(seg_len))
    alim_ref[...] = mag_l * jnp.sin(li * dt * float(seg_len))


def _s5_params(lam_re, lam_im, log_dt, b_re, b_im, seg_len):
    g, p = lam_re.shape
    h = b_re.shape[-1]
    vec = jax.ShapeDtypeStruct((g, 1, p), F32)
    mat = jax.ShapeDtypeStruct((g, h, p), F32)
    return pl.pallas_call(
        functools.partial(_s5_param_kernel, seg_len=seg_len),
        out_shape=(vec, vec, vec, vec, mat, mat),
        name="s5_params",
    )(lam_re.reshape(g, 1, p), lam_im.reshape(g, 1, p), log_dt.reshape(g, 1, 1),
      jnp.swapaxes(b_re, 1, 2), jnp.swapaxes(b_im, 1, 2))


def _s5_kernel(u_ref, perm_ref, permt_ref, b_ref, cre_ref, cim_ref, a_ref, d_ref,
               y_ref, bu_ref, carry_ref, *, seg_len, half, lane_split):
    @pl.when(pl.program_id(2) == 0)
    def _():
        carry_ref[...] = jnp.zeros_like(carry_ref)

    u = u_ref[...]
    up = _dot(perm_ref[...], u).astype(BF16)
    bu_ref[...] = _dot(up, b_ref[...])

    for s in range(half // lane_split):
        lo = s * lane_split
        re_l = pl.ds(lo, lane_split)
        im_l = pl.ds(half + lo, lane_split)
        a_re = jnp.broadcast_to(a_ref[0:1, re_l], (SUBLANES, lane_split))
        a_im = jnp.broadcast_to(a_ref[1:2, re_l], (SUBLANES, lane_split))

        def scan(h0, store, re_l=re_l, im_l=im_l, a_re=a_re, a_im=a_im):
            def step(i, carry):
                hr, hi = carry
                rows = pl.ds(pl.multiple_of(i * SUBLANES, SUBLANES), SUBLANES)
                nr = a_re * hr - a_im * hi + bu_ref[rows, re_l]
                ni = a_re * hi + a_im * hr + bu_ref[rows, im_l]
                if store:
                    bu_ref[rows, re_l] = nr
                    bu_ref[rows, im_l] = ni
                return nr, ni
            return lax.fori_loop(0, seg_len, step, h0, unroll=4)

        zero = jnp.zeros((SUBLANES, lane_split), F32)
        fin_r, fin_i = scan((zero, zero), False)

        al_re = a_ref[2:3, re_l]
        al_im = a_ref[3:4, re_l]
        c_r = carry_ref[0:1, re_l]
        c_i = carry_ref[1:2, re_l]
        rows8 = lax.broadcasted_iota(jnp.int32, (SUBLANES, lane_split), 0)
        hin_r = zero
        hin_i = zero
        for k in range(SUBLANES):
            hin_r = jnp.where(rows8 == k, c_r, hin_r)
            hin_i = jnp.where(rows8 == k, c_i, hin_i)
            c_r, c_i = (al_re * c_r - al_im * c_i + fin_r[k:k + 1, :],
                        al_re * c_i + al_im * c_r + fin_i[k:k + 1, :])
        carry_ref[0:1, re_l] = c_r
        carry_ref[1:2, re_l] = c_i

        scan((hin_r, hin_i), True)

    xr = bu_ref[:, 0:half].astype(BF16)
    xi = bu_ref[:, half:2 * half].astype(BF16)
    yp = _dot(xr, cre_ref[...]) - _dot(xi, cim_ref[...])
    yp_hi = yp.astype(BF16)
    yp_lo = (yp - yp_hi.astype(F32)).astype(BF16)
    y = _dot(permt_ref[...], yp_hi) + _dot(permt_ref[...], yp_lo)
    y_ref[...] = y + d_ref[...] * u.astype(F32)


def _s5_scan(h0, a_re, a_im, al_re, al_im, bbt_re, bbt_im, c_re, c_im, d_skip, main_width):
    bsz, seq, _ = h0.shape
    groups = a_re.shape[0]
    gc = S5_CHUNK_GROUPS
    n_chunks = groups // gc
    cin = gc * SSM_GROUP
    half = gc * SSM_STATE
    tm = min(S5_TIME_TILE, seq)
    seg_len = tm // SUBLANES

    eye = jnp.eye(gc, dtype=F32)

    def blockdiag_in(bbt):
        x = bbt.reshape(n_chunks, gc, SSM_GROUP, SSM_STATE)
        return jnp.einsum('cgip,gk->cgikp', x, eye).reshape(n_chunks, cin, half)

    def blockdiag_out(c):
        x = c.reshape(n_chunks, gc, SSM_GROUP, SSM_STATE)
        return jnp.einsum('cghp,gk->cgpkh', x, eye).reshape(n_chunks, half, cin)

    b_bd = jnp.concatenate([blockdiag_in(bbt_re), blockdiag_in(bbt_im)], axis=-1).astype(BF16)
    cre_bd = blockdiag_out(c_re).astype(BF16)
    cim_bd = blockdiag_out(c_im).astype(BF16)
    a4 = jnp.concatenate([v.reshape(n_chunks, 1, half) for v in (a_re, a_im, al_re, al_im)], axis=1)
    dsk = d_skip.reshape(n_chunks, 1, cin)

    r = np.arange(tm)
    perm_np = np.zeros((tm, tm), np.float32)
    perm_np[r, (r % SUBLANES) * seg_len + r // SUBLANES] = 1.0
    perm = jnp.asarray(perm_np, BF16)
    permt = jnp.asarray(perm_np.T, BF16)

    return pl.pallas_call(
        functools.partial(_s5_kernel, seg_len=seg_len, half=half, lane_split=min(S5_LANE_SPLIT, half)),
        out_shape=jax.ShapeDtypeStruct((bsz, seq, main_width), F32),
        grid=(bsz, n_chunks, seq // tm),
        in_specs=[pl.BlockSpec((None, tm, cin), lambda b, c, t: (b, t, c)),
                  pl.BlockSpec((tm, tm), lambda b, c, t: (0, 0)),
                  pl.BlockSpec((tm, tm), lambda b, c, t: (0, 0)),
                  pl.BlockSpec((None, cin, 2 * half), lambda b, c, t: (c, 0, 0)),
                  pl.BlockSpec((None, half, cin), lambda b, c, t: (c, 0, 0)),
                  pl.BlockSpec((None, half, cin), lambda b, c, t: (c, 0, 0)),
                  pl.BlockSpec((None, 4, half), lambda b, c, t: (c, 0, 0)),
                  pl.BlockSpec((None, 1, cin), lambda b, c, t: (c, 0, 0))],
        out_specs=pl.BlockSpec((None, tm, cin), lambda b, c, t: (b, t, c)),
        scratch_shapes=[pltpu.VMEM((tm, 2 * half), F32),
                        pltpu.VMEM((2, half), F32)],
        compiler_params=_cparams(("parallel", "parallel", "arbitrary")),
        name="s5_scan",
    )(h0, perm, permt, b_bd, cre_bd, cim_bd, a4, dsk)


def _glu_kernel(y_ref, w_ref, b_ref, o_ref):
    y = y_ref[...]
    g = y * (0.5 * (1.0 + jnp.tanh(math.sqrt(2.0 / math.pi) * (y + 0.044715 * (y * y * y)))))
    s = _dot(g.astype(BF16), w_ref[...]) + b_ref[...]
    o_ref[...] = (g * _sigmoid(s)).astype(o_ref.dtype)


def _glu(y2d, w_bf16, b_row):
    m, n = y2d.shape
    bm = min(512, m)
    return pl.pallas_call(
        _glu_kernel,
        out_shape=jax.ShapeDtypeStruct((m, n), BF16),
        grid=(m // bm,),
        in_specs=[pl.BlockSpec((bm, n), lambda i: (i, 0)),
                  pl.BlockSpec((n, n), lambda i: (0, 0)),
                  pl.BlockSpec((1, n), lambda i: (0, 0))],
        out_specs=pl.BlockSpec((bm, n), lambda i: (i, 0)),
        compiler_params=_cparams(("parallel",)),
        name="s5_glu",
    )(y2d, w_bf16, b_row)


def _tail_kernel(main_ref, z_ref, mq_ref, mz_ref, kv_ref, wm_ref, wmem_ref, x_ref, g_ref, b_ref,
                 o_ref, *, alpha):
    z = z_ref[...].astype(F32)
    a_main = (main_ref[...].astype(F32) * (z * _sigmoid(z))).astype(BF16)

    mq = mq_ref[...]
    kv = kv_ref[...]
    mem_w = MEM_HEADS * MEM_HEAD_DIM
    outs = []
    for h in range(MEM_HEADS):
        q = mq[:, h * MEM_HEAD_DIM:(h + 1) * MEM_HEAD_DIM]
        k = kv[:, h * MEM_HEAD_DIM:(h + 1) * MEM_HEAD_DIM]
        v = kv[:, mem_w + h * MEM_HEAD_DIM:mem_w + (h + 1) * MEM_HEAD_DIM]
        s = lax.dot_general(q, k, (((1,), (1,)), ((), ())),
                            preferred_element_type=F32) * (MEM_HEAD_DIM ** -0.5)
        p = jnp.exp(s - jnp.max(s, axis=-1, keepdims=True))
        l = jnp.sum(p, axis=-1, keepdims=True)
        outs.append(_dot(p.astype(BF16), v) / l)
    mz = mz_ref[...].astype(F32)
    memo = (jnp.concatenate(outs, axis=-1) * (mz * _sigmoid(mz))).astype(BF16)

    y = _dot(a_main, wm_ref[...]) + _dot(memo, wmem_ref[...])
    r = alpha * x_ref[...] + y
    mu = jnp.mean(r, axis=-1, keepdims=True)
    d = r - mu
    var = jnp.mean(d * d, axis=-1, keepdims=True)
    o_ref[...] = d * lax.rsqrt(var + LN_EPS) * g_ref[...] + b_ref[...]


def _layer_tail(main, h, z_blk, mq_blk, mz_blk, kv_mem, w_out, x, ln_g, ln_b, alpha):
    bsz, seq, d_model = x.shape
    main_w = main.shape[-1]
    mem_w = d_model - main_w
    bm = min(256, seq)
    w_main = w_out[:main_w].astype(BF16)
    w_mem = w_out[main_w:].astype(BF16)
    return pl.pallas_call(
        functools.partial(_tail_kernel, alpha=alpha),
        out_shape=jax.ShapeDtypeStruct((bsz, seq, d_model), F32),
        grid=(bsz, seq // bm),
        in_specs=[pl.BlockSpec((None, bm, main_w), lambda b, i: (b, i, 0)),
                  pl.BlockSpec((None, bm, main_w), lambda b, i: (b, i, z_blk)),
                  pl.BlockSpec((None, bm, mem_w), lambda b, i: (b, i, mq_blk)),
                  pl.BlockSpec((None, bm, mem_w), lambda b, i: (b, i, mz_blk)),
                  pl.BlockSpec((None,) + kv_mem.shape[1:], lambda b, i: (b, 0, 0)),
                  pl.BlockSpec((main_w, d_model), lambda b, i: (0, 0)),
                  pl.BlockSpec((mem_w, d_model), lambda b, i: (0, 0)),
                  pl.BlockSpec((None, bm, d_model), lambda b, i: (b, i, 0)),
                  pl.BlockSpec((1, d_model), lambda b, i: (0, 0)),
                  pl.BlockSpec((1, d_model), lambda b, i: (0, 0))],
        out_specs=pl.BlockSpec((None, bm, d_model), lambda b, i: (b, i, 0)),
        compiler_params=_cparams(("parallel", "parallel")),
        name="layer_tail",
    )(main, h, h, h, kv_mem, w_main, w_mem, x, ln_g.reshape(1, d_model), ln_b.reshape(1, d_model))


def _rms(c, g):
    return c * lax.rsqrt(jnp.mean(c * c, axis=-1, keepdims=True) + RMS_EPS) * g


def _rope_rotate(pair, cos, sin):
    return pair * cos + pltpu.roll(pair, MLA_ROPE, 1) * sin


def _kv_kernel(x_ref, wd_ref, g_ref, wkr_ref, wukv_ref, cos_ref, sin_ref, k_ref, v_ref, *, heads):
    xb = x_ref[...].astype(BF16)
    c = _rms(_dot(xb, wd_ref[...]), g_ref[...]).astype(BF16)
    krot = _rope_rotate(_dot(xb, wkr_ref[...]), cos_ref[...], sin_ref[...])[:, 0:MLA_ROPE].astype(BF16)
    kv = _dot(c, wukv_ref[...])
    wh = MLA_NOPE + MLA_V_DIM
    for h in range(heads):
        k_ref[h, :, 0:MLA_NOPE] = kv[:, h * wh:h * wh + MLA_NOPE].astype(BF16)
        k_ref[h, :, MLA_NOPE:MLA_QK] = krot
        v_ref[:, h * MLA_V_DIM:(h + 1) * MLA_V_DIM] = kv[:, h * wh + MLA_NOPE:(h + 1) * wh].astype(BF16)


def _swap_halves(w):
    half = w.shape[-1] // 2
    return jnp.concatenate([w[..., half:], w[..., :half]], axis=-1)


def _kv_shared(x, w_dkv, kv_norm_g, w_kr, w_uk, w_uv, cos, sin):
    bsz, seq, d_model = x.shape
    lora, heads, _ = w_uk.shape
    bm = min(512, seq)
    w_kr2 = jnp.concatenate([w_kr, _swap_halves(w_kr)], axis=-1).astype(BF16)
    w_ukv = jnp.concatenate([w_uk, w_uv], axis=-1).reshape(lora, heads * (MLA_NOPE + MLA_V_DIM)).astype(BF16)
    return pl.pallas_call(
        functools.partial(_kv_kernel, heads=heads),
        out_shape=(jax.ShapeDtypeStruct((bsz, heads, seq, MLA_QK), BF16),
                   jax.ShapeDtypeStruct((bsz, seq, heads * MLA_V_DIM), BF16)),
        grid=(bsz, seq // bm),
        in_specs=[pl.BlockSpec((None, bm, d_model), lambda b, i: (b, i, 0)),
                  pl.BlockSpec((d_model, lora), lambda b, i: (0, 0)),
                  pl.BlockSpec((1, lora), lambda b, i: (0, 0)),
                  pl.BlockSpec((d_model, LANES), lambda b, i: (0, 0)),
                  pl.BlockSpec(w_ukv.shape, lambda b, i: (0, 0)),
                  pl.BlockSpec((None, bm, LANES), lambda b, i: (b, i, 0)),
                  pl.BlockSpec((None, bm, LANES), lambda b, i: (b, i, 0))],
        out_specs=(pl.BlockSpec((None, heads, bm, MLA_QK), lambda b, i: (b, 0, i, 0)),
                   pl.BlockSpec((None, bm, heads * MLA_V_DIM), lambda b, i: (b, i, 0))),
        compiler_params=_cparams(("parallel", "parallel")),
        name="mla_kv",
    )(x, w_dkv.astype(BF16), kv_norm_g.reshape(1, lora), w_kr2, w_ukv, cos, sin)


def _q_kernel(cq_ref, g_ref, w_ref, cos_ref, sin_ref, q_ref, *, heads, scale):
    c = _rms(cq_ref[...].astype(F32), g_ref[...]).astype(BF16)
    qa = _dot(c, w_ref[...])
    cos = cos_ref[...]
    sin = sin_ref[...]
    wh = MLA_NOPE + 2 * MLA_ROPE
    for h in range(heads):
        q_ref[h, :, 0:MLA_NOPE] = (qa[:, h * wh:h * wh + MLA_NOPE] * scale).astype(BF16)
        rot = _rope_rotate(qa[:, h * wh + MLA_NOPE:(h + 1) * wh], cos, sin)
        q_ref[h, :, MLA_NOPE:MLA_QK] = (rot[:, 0:MLA_ROPE] * scale).astype(BF16)


def _q_proj(h, cq_blk, q_norm_g, w_uq, cos, sin):
    bsz, seq, _ = h.shape
    lora, heads, _ = w_uq.shape
    bm = min(512, seq)
    rope_w = w_uq[..., MLA_NOPE:]
    w2 = jnp.concatenate([w_uq, _swap_halves(rope_w)], axis=-1)
    w2 = w2.reshape(lora, heads * (MLA_NOPE + 2 * MLA_ROPE)).astype(BF16)
    return pl.pallas_call(
        functools.partial(_q_kernel, heads=heads, scale=MLA_QK ** -0.5),
        out_shape=jax.ShapeDtypeStruct((bsz, heads, seq, MLA_QK), BF16),
        grid=(bsz, seq // bm),
        in_specs=[pl.BlockSpec((None, bm, lora), lambda b, i: (b, i, cq_blk)),
                  pl.BlockSpec((1, lora), lambda b, i: (0, 0)),
                  pl.BlockSpec(w2.shape, lambda b, i: (0, 0)),
                  pl.BlockSpec((None, bm, LANES), lambda b, i: (b, i, 0)),
                  pl.BlockSpec((None, bm, LANES), lambda b, i: (b, i, 0))],
        out_specs=pl.BlockSpec((None, heads, bm, MLA_QK), lambda b, i: (b, 0, i, 0)),
        compiler_params=_cparams(("parallel", "parallel")),
        name="mla_q",
    )(h, q_norm_g.reshape(1, lora), w2, cos, sin)


_FIRST, _LAST, _MASKED = 1, 2, 4


def _flash_kernel(qi_ref, kj_ref, fl_ref, q_ref, k_ref, v_ref, o_ref, m_ref, l_ref, acc_ref, *, tq, tk):
    n = pl.program_id(2)
    flags = fl_ref[n]

    @pl.when((flags & _FIRST) != 0)
    def _():
        m_ref[...] = jnp.full(m_ref.shape, NEG_INF, F32)
        l_ref[...] = jnp.zeros_like(l_ref)
        acc_ref[...] = jnp.zeros_like(acc_ref)

    def update(masked):
        s = lax.dot_general(q_ref[...], k_ref[...], (((1,), (1,)), ((), ())),
                            preferred_element_type=F32)
        if masked:
            row = qi_ref[n] * tq + lax.broadcasted_iota(jnp.int32, (tq, tk), 0)
            col = kj_ref[n] * tk + lax.broadcasted_iota(jnp.int32, (tq, tk), 1)
            s = jnp.where(col <= row, s, NEG_INF)
        m_prev = m_ref[...]
        m_new = jnp.maximum(m_prev, jnp.max(s, axis=-1, keepdims=True))
        alpha = jnp.exp(m_prev - m_new)
        p = jnp.exp(s - pltpu.repeat(m_new, tk // LANES, axis=1))
        l_ref[...] = alpha * l_ref[...] + jnp.sum(p, axis=-1, keepdims=True)
        acc_ref[...] = alpha * acc_ref[...] + _dot(p.astype(BF16), v_ref[...])
        m_ref[...] = m_new

    @pl.when((flags & _MASKED) == 0)
    def _():
        update(False)

    @pl.when((flags & _MASKED) != 0)
    def _():
        update(True)

    @pl.when((flags & _LAST) != 0)
    def _():
        o_ref[...] = (acc_ref[...] / l_ref[...]).astype(o_ref.dtype)


def _flash_attention(q, k, v):
    bsz, heads, seq, dk = q.shape
    tq = min(FLASH_TQ, seq)
    tk = min(FLASH_TK, seq)
    qi, kj, fl = [], [], []
    for i in range(seq // tq):
        n_kv = ((i + 1) * tq) // tk
        for j in range(n_kv):
            qi.append(i)
            kj.append(j)
            masked = (j + 1) * tk - 1 > i * tq
            fl.append((_FIRST if j == 0 else 0) | (_LAST if j == n_kv - 1 else 0) | (_MASKED if masked else 0))
    qi = jnp.asarray(qi, jnp.int32)
    kj = jnp.asarray(kj, jnp.int32)
    fl = jnp.asarray(fl, jnp.int32)
    grid_spec = pltpu.PrefetchScalarGridSpec(
        num_scalar_prefetch=3,
        grid=(bsz, heads, int(qi.shape[0])),
        in_specs=[pl.BlockSpec((None, None, tq, dk), lambda b, h, n, qi, kj, fl: (b, h, qi[n], 0)),
                  pl.BlockSpec((None, None, tk, dk), lambda b, h, n, qi, kj, fl: (b, h, kj[n], 0)),
                  pl.BlockSpec((None, tk, MLA_V_DIM), lambda b, h, n, qi, kj, fl: (b, kj[n], h))],
        out_specs=pl.BlockSpec((None, tq, MLA_V_DIM), lambda b, h, n, qi, kj, fl: (b, qi[n], h)),
        scratch_shapes=[pltpu.VMEM((tq, LANES), F32),
                        pltpu.VMEM((tq, LANES), F32),
                        pltpu.VMEM((tq, MLA_V_DIM), F32)],
    )
    return pl.pallas_call(
        functools.partial(_flash_kernel, tq=tq, tk=tk),
        out_shape=jax.ShapeDtypeStruct((bsz, seq, heads * MLA_V_DIM), BF16),
        grid_spec=grid_spec,
        compiler_params=_cparams(("parallel", "parallel", "arbitrary")),
        name="mla_flash",
    )(qi, kj, fl, q, k, v)


def kernel(x, mem, positions, w_in_a, lam_re, lam_im, log_dt, b_re, b_im, c_re, c_im, d_skip,
           w_glu, b_glu, w_in_b, q_norm_g, w_uq, w_dkv, kv_norm_g, w_kr, w_uk, w_uv, w_mem_kv,
           w_out, ln_g, ln_b):
    bsz, seq, d_model = x.shape
    n_ssm = w_in_a.shape[0]
    n_mla = w_in_b.shape[0]
    depth = n_ssm + n_mla
    main_w = w_glu.shape[-1]
    mem_w = d_model - main_w
    q_lora = q_norm_g.shape[-1]
    alpha = (2.0 * depth) ** 0.25
    tokens = bsz * seq
    assert main_w % mem_w == 0 and q_lora == mem_w

    cos, sin = _rope_tables(positions)
    mem2d = mem.reshape(bsz * mem.shape[1], d_model)
    seg_len = min(S5_TIME_TILE, seq) // SUBLANES

    k_shared = v_shared = None
    for layer in range(depth):
        kv_mem = _inproj(mem2d, w_mem_kv[layer].astype(BF16), "mem_kv").reshape(bsz, mem.shape[1], 2 * mem_w)
        x2d = x.reshape(tokens, d_model)
        if layer < n_ssm:
            i = layer
            h = _inproj(x2d, w_in_a[i].astype(BF16), "inproj_a").reshape(bsz, seq, -1)
            a_re, a_im, al_re, al_im, bbt_re, bbt_im = _s5_params(
                lam_re[i], lam_im[i], log_dt[i], b_re[i], b_im[i], seg_len)
            y = _s5_scan(h, a_re, a_im, al_re, al_im, bbt_re, bbt_im, c_re[i], c_im[i], d_skip[i], main_w)
            main = _glu(y.reshape(tokens, main_w), w_glu[i].astype(BF16),
                        b_glu[i].reshape(1, main_w)).reshape(bsz, seq, main_w)
            z_blk = 1
            mq_blk = 2 * main_w // mem_w
        else:
            j = layer - n_ssm
            w_b = jnp.concatenate([w_in_b[j][:, q_lora:q_lora + main_w], w_in_b[j][:, :q_lora],
                                   w_in_b[j][:, q_lora + main_w:]], axis=-1).astype(BF16)
            h = _inproj(x2d, w_b, "inproj_b").reshape(bsz, seq, -1)
            q = _q_proj(h, main_w // q_lora, q_norm_g[j], w_uq[j], cos, sin)
            main = _flash_attention(q, k_shared, v_shared)
            z_blk = 0
            mq_blk = (main_w + q_lora) // mem_w
        x = _layer_tail(main, h, z_blk, mq_blk, mq_blk + 1, kv_mem, w_out[layer], x,
                        ln_g[layer], ln_b[layer], alpha)
        if layer == n_ssm - 1:
            k_shared, v_shared = _kv_shared(x, w_dkv, kv_norm_g, w_kr, w_uk, w_uv, cos, sin)
    return x
```

```python
import functools
import math

import numpy as np
import jax
import jax.numpy as jnp
from jax import lax
from jax.experimental import pallas as pl
from jax.experimental.pallas import tpu as pltpu

F32 = jnp.float32
BF16 = jnp.bfloat16

MEM_HEADS = 4
MEM_HEAD_DIM = 128
SSM_GROUP = 16
SSM_STATE = 64
MLA_V_DIM = 128
MLA_NOPE = 128
MLA_ROPE = 64
MLA_QK = MLA_NOPE + MLA_ROPE
ROPE_BASE = 10000.0
LN_EPS = 1e-5
RMS_EPS = 1e-6
NEG_INF = -1e30

LANES = 128
SUBLANES = 8
VMEM_LIMIT = 52 * 1024 * 1024
S5_CHUNK_GROUPS = 16
S5_LANE_SPLIT = 512
S5_TIME_TILE = 512
FLASH_TQ = 1024
FLASH_TK = 1024
FLASH_HEADS = 6


def _cparams(sem):
    return pltpu.CompilerParams(dimension_semantics=sem, vmem_limit_bytes=VMEM_LIMIT)


def _dot(a, b):
    return jnp.dot(a, b, preferred_element_type=F32)


def _sigmoid(x):
    return 1.0 / (1.0 + jnp.exp(-x))


def _rope_table_kernel(pos_ref, invf_ref, sign_ref, cos_ref, sin_ref):
    ang = pos_ref[...].astype(F32) * invf_ref[...]
    cos_ref[...] = jnp.cos(ang)
    sin_ref[...] = jnp.sin(ang) * sign_ref[...]


def _rope_tables(positions):
    bsz, seq = positions.shape
    tokens = bsz * seq
    bm = min(2048, tokens)
    inv_freq = ROPE_BASE ** (-jnp.arange(0, MLA_ROPE, 2, dtype=F32) / MLA_ROPE)
    invf = jnp.tile(inv_freq, 4).reshape(1, LANES)
    sign = jnp.tile(jnp.concatenate([-jnp.ones((MLA_ROPE // 2,), F32),
                                     jnp.ones((MLA_ROPE // 2,), F32)]), 2).reshape(1, LANES)
    cos, sin = pl.pallas_call(
        _rope_table_kernel,
        out_shape=(jax.ShapeDtypeStruct((tokens, LANES), F32),
                   jax.ShapeDtypeStruct((tokens, LANES), F32)),
        grid=(tokens // bm,),
        in_specs=[pl.BlockSpec((bm, 1), lambda i: (i, 0)),
                  pl.BlockSpec((1, LANES), lambda i: (0, 0)),
                  pl.BlockSpec((1, LANES), lambda i: (0, 0))],
        out_specs=(pl.BlockSpec((bm, LANES), lambda i: (i, 0)),
                   pl.BlockSpec((bm, LANES), lambda i: (i, 0))),
        compiler_params=_cparams(("parallel",)),
        name="rope_tables",
    )(positions.reshape(tokens, 1), invf, sign)
    return cos.reshape(bsz, seq, LANES), sin.reshape(bsz, seq, LANES)


def _inproj_kernel(x_ref, w_ref, o_ref, xb_ref):
    @pl.when(pl.program_id(1) == 0)
    def _():
        xb_ref[...] = x_ref[...].astype(BF16)

    o_ref[...] = _dot(xb_ref[...], w_ref[...]).astype(o_ref.dtype)


def _inproj(x2d, w_bf16, name):
    m, k = x2d.shape
    n = w_bf16.shape[1]
    bm = min(512, m)
    bn = min(1024, n)
    return pl.pallas_call(
        _inproj_kernel,
        out_shape=jax.ShapeDtypeStruct((m, n), BF16),
        grid=(m // bm, n // bn),
        in_specs=[pl.BlockSpec((bm, k), lambda i, j: (i, 0)),
                  pl.BlockSpec((k, bn), lambda i, j: (0, j))],
        out_specs=pl.BlockSpec((bm, bn), lambda i, j: (i, j)),
        scratch_shapes=[pltpu.VMEM((bm, k), BF16)],
        compiler_params=_cparams(("parallel", "arbitrary")),
        name=name,
    )(x2d, w_bf16)


def _s5_param_kernel(lr_ref, li_ref, ldt_ref, br_ref, bi_ref,
                     are_ref, aim_ref, alre_ref, alim_ref, bbr_ref, bbi_ref, *, seg_len):
    lr = lr_ref[...]
    li = li_ref[...]
    dt = jnp.exp(ldt_ref[...])
    mag = jnp.exp(lr * dt)
    a_re = mag * jnp.cos(li * dt)
    a_im = mag * jnp.sin(li * dt)
    den = lr * lr + li * li
    n_re = a_re - 1.0
    n_im = a_im
    coef_re = (n_re * lr + n_im * li) / den
    coef_im = (n_im * lr - n_re * li) / den
    br = br_ref[...]
    bi = bi_ref[...]
    bbr_ref[...] = coef_re * br - coef_im * bi
    bbi_ref[...] = coef_re * bi + coef_im * br
    are_ref[...] = a_re
    aim_ref[...] = a_im
    mag_l = jnp.exp(lr * dt * float(seg_len))
    alre_ref[...] = mag_l * jnp.cos(li * dt * float(seg_len))
    alim_ref[...] = mag_l * jnp.sin(li * dt * float(seg_len))


def _s5_params(lam_re, lam_im, log_dt, b_re, b_im, seg_len):
    g, p = lam_re.shape
    h = b_re.shape[-1]
    vec = jax.ShapeDtypeStruct((g, 1, p), F32)
    mat = jax.ShapeDtypeStruct((g, h, p), F32)
    return pl.pallas_call(
        functools.partial(_s5_param_kernel, seg_len=seg_len),
        out_shape=(vec, vec, vec, vec, mat, mat),
        name="s5_params",
    )(lam_re.reshape(g, 1, p), lam_im.reshape(g, 1, p), log_dt.reshape(g, 1, 1),
      jnp.swapaxes(b_re, 1, 2), jnp.swapaxes(b_im, 1, 2))


def _s5_kernel(u_ref, perm_ref, permt_ref, b_ref, cre_ref, cim_ref, a_ref, d_ref,
               y_ref, bu_ref, carry_ref, *, seg_len, half, lane_split):
    @pl.when(pl.program_id(2) == 0)
    def _():
        carry_ref[...] = jnp.zeros_like(carry_ref)

    u = u_ref[...]
    up = _dot(perm_ref[...], u).astype(BF16)
    bu_ref[...] = _dot(up, b_ref[...])

    for s in range(half // lane_split):
        lo = s * lane_split
        re_l = pl.ds(lo, lane_split)
        im_l = pl.ds(half + lo, lane_split)
        a_re = jnp.broadcast_to(a_ref[0:1, re_l], (SUBLANES, lane_split))
        a_im = jnp.broadcast_to(a_ref[1:2, re_l], (SUBLANES, lane_split))

        def scan(h0, store, re_l=re_l, im_l=im_l, a_re=a_re, a_im=a_im):
            def step(i, carry):
                hr, hi = carry
                rows = pl.ds(pl.multiple_of(i * SUBLANES, SUBLANES), SUBLANES)
                nr = a_re * hr - a_im * hi + bu_ref[rows, re_l]
                ni = a_re * hi + a_im * hr + bu_ref[rows, im_l]
                if store:
                    bu_ref[rows, re_l] = nr
                    bu_ref[rows, im_l] = ni
                return nr, ni
            return lax.fori_loop(0, seg_len, step, h0, unroll=4)

        zero = jnp.zeros((SUBLANES, lane_split), F32)
        fin_r, fin_i = scan((zero, zero), False)

        al_re = a_ref[2:3, re_l]
        al_im = a_ref[3:4, re_l]
        c_r = carry_ref[0:1, re_l]
        c_i = carry_ref[1:2, re_l]
        rows8 = lax.broadcasted_iota(jnp.int32, (SUBLANES, lane_split), 0)
        hin_r = zero
        hin_i = zero
        for k in range(SUBLANES):
            hin_r = jnp.where(rows8 == k, c_r, hin_r)
            hin_i = jnp.where(rows8 == k, c_i, hin_i)
            c_r, c_i = (al_re * c_r - al_im * c_i + fin_r[k:k + 1, :],
                        al_re * c_i + al_im * c_r + fin_i[k:k + 1, :])
        carry_ref[0:1, re_l] = c_r
        carry_ref[1:2, re_l] = c_i

        scan((hin_r, hin_i), True)

    xr = bu_ref[:, 0:half].astype(BF16)
    xi = bu_ref[:, half:2 * half].astype(BF16)
    yp = _dot(xr, cre_ref[...]) - _dot(xi, cim_ref[...])
    yp_hi = yp.astype(BF16)
    yp_lo = (yp - yp_hi.astype(F32)).astype(BF16)
    y = _dot(permt_ref[...], yp_hi) + _dot(permt_ref[...], yp_lo)
    y_ref[...] = y + d_ref[...] * u.astype(F32)


def _s5_scan(h0, a_re, a_im, al_re, al_im, bbt_re, bbt_im, c_re, c_im, d_skip, main_width):
    bsz, seq, _ = h0.shape
    groups = a_re.shape[0]
    gc = S5_CHUNK_GROUPS
    n_chunks = groups // gc
    cin = gc * SSM_GROUP
    half = gc * SSM_STATE
    tm = min(S5_TIME_TILE, seq)
    seg_len = tm // SUBLANES

    eye = jnp.eye(gc, dtype=F32)

    def blockdiag_in(bbt):
        x = bbt.reshape(n_chunks, gc, SSM_GROUP, SSM_STATE)
        return jnp.einsum('cgip,gk->cgikp', x, eye).reshape(n_chunks, cin, half)

    def blockdiag_out(c):
        x = c.reshape(n_chunks, gc, SSM_GROUP, SSM_STATE)
        return jnp.einsum('cghp,gk->cgpkh', x, eye).reshape(n_chunks, half, cin)

    b_bd = jnp.concatenate([blockdiag_in(bbt_re), blockdiag_in(bbt_im)], axis=-1).astype(BF16)
    cre_bd = blockdiag_out(c_re).astype(BF16)
    cim_bd = blockdiag_out(c_im).astype(BF16)
    a4 = jnp.concatenate([v.reshape(n_chunks, 1, half) for v in (a_re, a_im, al_re, al_im)], axis=1)
    dsk = d_skip.reshape(n_chunks, 1, cin)

    r = np.arange(tm)
    perm_np = np.zeros((tm, tm), np.float32)
    perm_np[r, (r % SUBLANES) * seg_len + r // SUBLANES] = 1.0
    perm = jnp.asarray(perm_np, BF16)
    permt = jnp.asarray(perm_np.T, BF16)

    return pl.pallas_call(
        functools.partial(_s5_kernel, seg_len=seg_len, half=half, lane_split=min(S5_LANE_SPLIT, half)),
        out_shape=jax.ShapeDtypeStruct((bsz, seq, main_width), F32),
        grid=(bsz, n_chunks, seq // tm),
        in_specs=[pl.BlockSpec((None, tm, cin), lambda b, c, t: (b, t, c)),
                  pl.BlockSpec((tm, tm), lambda b, c, t: (0, 0)),
                  pl.BlockSpec((tm, tm), lambda b, c, t: (0, 0)),
                  pl.BlockSpec((None, cin, 2 * half), lambda b, c, t: (c, 0, 0)),
                  pl.BlockSpec((None, half, cin), lambda b, c, t: (c, 0, 0)),
                  pl.BlockSpec((None, half, cin), lambda b, c, t: (c, 0, 0)),
                  pl.BlockSpec((None, 4, half), lambda b, c, t: (c, 0, 0)),
                  pl.BlockSpec((None, 1, cin), lambda b, c, t: (c, 0, 0))],
        out_specs=pl.BlockSpec((None, tm, cin), lambda b, c, t: (b, t, c)),
        scratch_shapes=[pltpu.VMEM((tm, 2 * half), F32),
                        pltpu.VMEM((2, half), F32)],
        compiler_params=_cparams(("parallel", "parallel", "arbitrary")),
        name="s5_scan",
    )(h0, perm, permt, b_bd, cre_bd, cim_bd, a4, dsk)


def _glu_kernel(y_ref, w_ref, b_ref, o_ref):
    y = y_ref[...]
    g = y * (0.5 * (1.0 + jnp.tanh(math.sqrt(2.0 / math.pi) * (y + 0.044715 * (y * y * y)))))
    s = _dot(g.astype(BF16), w_ref[...]) + b_ref[...]
    o_ref[...] = (g * _sigmoid(s)).astype(o_ref.dtype)


def _glu(y2d, w_bf16, b_row):
    m, n = y2d.shape
    bm = min(512, m)
    return pl.pallas_call(
        _glu_kernel,
        out_shape=jax.ShapeDtypeStruct((m, n), BF16),
        grid=(m // bm,),
        in_specs=[pl.BlockSpec((bm, n), lambda i: (i, 0)),
                  pl.BlockSpec((n, n), lambda i: (0, 0)),
                  pl.BlockSpec((1, n), lambda i: (0, 0))],
        out_specs=pl.BlockSpec((bm, n), lambda i: (i, 0)),
        compiler_params=_cparams(("parallel",)),
        name="s5_glu",
    )(y2d, w_bf16, b_row)


def _tail_kernel(main_ref, z_ref, mq_ref, mz_ref, kv_ref, wm_ref, wmem_ref, x_ref, g_ref, b_ref,
                 o_ref, *, alpha):
    z = z_ref[...].astype(F32)
    a_main = (main_ref[...].astype(F32) * (z * _sigmoid(z))).astype(BF16)

    mq = mq_ref[...]
    kv = kv_ref[...]
    mem_w = MEM_HEADS * MEM_HEAD_DIM
    outs = []
    for h in range(MEM_HEADS):
        q = mq[:, h * MEM_HEAD_DIM:(h + 1) * MEM_HEAD_DIM]
        k = kv[:, h * MEM_HEAD_DIM:(h + 1) * MEM_HEAD_DIM]
        v = kv[:, mem_w + h * MEM_HEAD_DIM:mem_w + (h + 1) * MEM_HEAD_DIM]
        s = lax.dot_general(q, k, (((1,), (1,)), ((), ())),
                            preferred_element_type=F32) * (MEM_HEAD_DIM ** -0.5)
        p = jnp.exp(s - jnp.max(s, axis=-1, keepdims=True))
        l = jnp.sum(p, axis=-1, keepdims=True)
        outs.append(_dot(p.astype(BF16), v) / l)
    mz = mz_ref[...].astype(F32)
    memo = (jnp.concatenate(outs, axis=-1) * (mz * _sigmoid(mz))).astype(BF16)

    y = _dot(a_main, wm_ref[...]) + _dot(memo, wmem_ref[...])
    r = alpha * x_ref[...] + y
    mu = jnp.mean(r, axis=-1, keepdims=True)
    d = r - mu
    var = jnp.mean(d * d, axis=-1, keepdims=True)
    o_ref[...] = d * lax.rsqrt(var + LN_EPS) * g_ref[...] + b_ref[...]


def _layer_tail(main, h, z_blk, mq_blk, mz_blk, kv_mem, w_out, x, ln_g, ln_b, alpha):
    bsz, seq, d_model = x.shape
    main_w = main.shape[-1]
    mem_w = d_model - main_w
    bm = min(256, seq)
    w_main = w_out[:main_w].astype(BF16)
    w_mem = w_out[main_w:].astype(BF16)
    return pl.pallas_call(
        functools.partial(_tail_kernel, alpha=alpha),
        out_shape=jax.ShapeDtypeStruct((bsz, seq, d_model), F32),
        grid=(bsz, seq // bm),
        in_specs=[pl.BlockSpec((None, bm, main_w), lambda b, i: (b, i, 0)),
                  pl.BlockSpec((None, bm, main_w), lambda b, i: (b, i, z_blk)),
                  pl.BlockSpec((None, bm, mem_w), lambda b, i: (b, i, mq_blk)),
                  pl.BlockSpec((None, bm, mem_w), lambda b, i: (b, i, mz_blk)),
                  pl.BlockSpec((None,) + kv_mem.shape[1:], lambda b, i: (b, 0, 0)),
                  pl.BlockSpec((main_w, d_model), lambda b, i: (0, 0)),
                  pl.BlockSpec((mem_w, d_model), lambda b, i: (0, 0)),
                  pl.BlockSpec((None, bm, d_model), lambda b, i: (b, i, 0)),
                  pl.BlockSpec((1, d_model), lambda b, i: (0, 0)),
                  pl.BlockSpec((1, d_model), lambda b, i: (0, 0))],
        out_specs=pl.BlockSpec((None, bm, d_model), lambda b, i: (b, i, 0)),
        compiler_params=_cparams(("parallel", "parallel")),
        name="layer_tail",
    )(main, h, h, h, kv_mem, w_main, w_mem, x, ln_g.reshape(1, d_model), ln_b.reshape(1, d_model))


def _rms(c, g):
    return c * lax.rsqrt(jnp.mean(c * c, axis=-1, keepdims=True) + RMS_EPS) * g


def _rope_rotate(pair, cos, sin):
    return pair * cos + pltpu.roll(pair, MLA_ROPE, 1) * sin


def _kv_kernel(x_ref, wd_ref, g_ref, wkr_ref, wukv_ref, cos_ref, sin_ref, k_ref, v_ref, *, heads):
    xb = x_ref[...].astype(BF16)
    c = _rms(_dot(xb, wd_ref[...]), g_ref[...]).astype(BF16)
    krot = _rope_rotate(_dot(xb, wkr_ref[...]), cos_ref[...], sin_ref[...])[:, 0:MLA_ROPE].astype(BF16)
    kv = _dot(c, wukv_ref[...])
    wh = MLA_NOPE + MLA_V_DIM
    for h in range(heads):
        k_ref[h, :, 0:MLA_NOPE] = kv[:, h * wh:h * wh + MLA_NOPE].astype(BF16)
        k_ref[h, :, MLA_NOPE:MLA_QK] = krot
        v_ref[:, h * MLA_V_DIM:(h + 1) * MLA_V_DIM] = kv[:, h * wh + MLA_NOPE:(h + 1) * wh].astype(BF16)


def _swap_halves(w):
    half = w.shape[-1] // 2
    return jnp.concatenate([w[..., half:], w[..., :half]], axis=-1)


def _kv_shared(x, w_dkv, kv_norm_g, w_kr, w_uk, w_uv, cos, sin):
    bsz, seq, d_model = x.shape
    lora, heads, _ = w_uk.shape
    bm = min(512, seq)
    w_kr2 = jnp.concatenate([w_kr, _swap_halves(w_kr)], axis=-1).astype(BF16)
    w_ukv = jnp.concatenate([w_uk, w_uv], axis=-1).reshape(lora, heads * (MLA_NOPE + MLA_V_DIM)).astype(BF16)
    return pl.pallas_call(
        functools.partial(_kv_kernel, heads=heads),
        out_shape=(jax.ShapeDtypeStruct((bsz, heads, seq, MLA_QK), BF16),
                   jax.ShapeDtypeStruct((bsz, seq, heads * MLA_V_DIM), BF16)),
        grid=(bsz, seq // bm),
        in_specs=[pl.BlockSpec((None, bm, d_model), lambda b, i: (b, i, 0)),
                  pl.BlockSpec((d_model, lora), lambda b, i: (0, 0)),
                  pl.BlockSpec((1, lora), lambda b, i: (0, 0)),
                  pl.BlockSpec((d_model, LANES), lambda b, i: (0, 0)),
                  pl.BlockSpec(w_ukv.shape, lambda b, i: (0, 0)),
                  pl.BlockSpec((None, bm, LANES), lambda b, i: (b, i, 0)),
                  pl.BlockSpec((None, bm, LANES), lambda b, i: (b, i, 0))],
        out_specs=(pl.BlockSpec((None, heads, bm, MLA_QK), lambda b, i: (b, 0, i, 0)),
                   pl.BlockSpec((None, bm, heads * MLA_V_DIM), lambda b, i: (b, i, 0))),
        compiler_params=_cparams(("parallel", "parallel")),
        name="mla_kv",
    )(x, w_dkv.astype(BF16), kv_norm_g.reshape(1, lora), w_kr2, w_ukv, cos, sin)


def _q_kernel(cq_ref, g_ref, w_ref, cos_ref, sin_ref, q_ref, *, heads, scale):
    c = _rms(cq_ref[...].astype(F32), g_ref[...]).astype(BF16)
    qa = _dot(c, w_ref[...])
    cos = cos_ref[...]
    sin = sin_ref[...]
    wh = MLA_NOPE + 2 * MLA_ROPE
    for h in range(heads):
        q_ref[h, :, 0:MLA_NOPE] = (qa[:, h * wh:h * wh + MLA_NOPE] * scale).astype(BF16)
        rot = _rope_rotate(qa[:, h * wh + MLA_NOPE:(h + 1) * wh], cos, sin)
        q_ref[h, :, MLA_NOPE:MLA_QK] = (rot[:, 0:MLA_ROPE] * scale).astype(BF16)


def _q_proj(h, cq_blk, q_norm_g, w_uq, cos, sin):
    bsz, seq, _ = h.shape
    lora, heads, _ = w_uq.shape
    bm = min(512, seq)
    rope_w = w_uq[..., MLA_NOPE:]
    w2 = jnp.concatenate([w_uq, _swap_halves(rope_w)], axis=-1)
    w2 = w2.reshape(lora, heads * (MLA_NOPE + 2 * MLA_ROPE)).astype(BF16)
    return pl.pallas_call(
        functools.partial(_q_kernel, heads=heads, scale=MLA_QK ** -0.5 * math.log2(math.e)),
        out_shape=jax.ShapeDtypeStruct((bsz, heads, seq, MLA_QK), BF16),
        grid=(bsz, seq // bm),
        in_specs=[pl.BlockSpec((None, bm, lora), lambda b, i: (b, i, cq_blk)),
                  pl.BlockSpec((1, lora), lambda b, i: (0, 0)),
                  pl.BlockSpec(w2.shape, lambda b, i: (0, 0)),
                  pl.BlockSpec((None, bm, LANES), lambda b, i: (b, i, 0)),
                  pl.BlockSpec((None, bm, LANES), lambda b, i: (b, i, 0))],
        out_specs=pl.BlockSpec((None, heads, bm, MLA_QK), lambda b, i: (b, 0, i, 0)),
        compiler_params=_cparams(("parallel", "parallel")),
        name="mla_q",
    )(h, q_norm_g.reshape(1, lora), w2, cos, sin)


_FIRST, _LAST, _MASKED = 1, 2, 4


def _flash_kernel(qi_ref, kj_ref, fl_ref, q_ref, k_ref, v_ref, o_ref, m_ref, acc_ref, *, tq, tk, hp):
    n = pl.program_id(2)
    flags = fl_ref[n]

    @pl.when((flags & _FIRST) != 0)
    def _():
        m_ref[...] = jnp.full(m_ref.shape, NEG_INF, F32)
        acc_ref[...] = jnp.zeros_like(acc_ref)

    def update(masked):
        if masked:
            row = qi_ref[n] * tq + lax.broadcasted_iota(jnp.int32, (tq, tk), 0)
            col = kj_ref[n] * tk + lax.broadcasted_iota(jnp.int32, (tq, tk), 1)
            keep = col <= row
        ones = jnp.ones((tk, LANES), BF16)
        for h in range(hp):
            s = lax.dot_general(q_ref[h], k_ref[h], (((1,), (1,)), ((), ())),
                                preferred_element_type=F32)
            if masked:
                s = jnp.where(keep, s, NEG_INF)
            m_prev = m_ref[h]
            m_new = jnp.maximum(m_prev, jnp.max(s, axis=-1, keepdims=True))
            alpha = jnp.exp2(m_prev - m_new)
            p = jnp.exp2(s - pltpu.repeat(m_new, tk // LANES, axis=1)).astype(BF16)
            v1 = jnp.concatenate([v_ref[:, h * MLA_V_DIM:(h + 1) * MLA_V_DIM], ones], axis=1)
            acc_ref[h] = pltpu.repeat(alpha, 2, axis=1) * acc_ref[h] + _dot(p, v1)
            m_ref[h] = m_new

    @pl.when((flags & _MASKED) == 0)
    def _():
        update(False)

    @pl.when((flags & _MASKED) != 0)
    def _():
        update(True)

    @pl.when((flags & _LAST) != 0)
    def _():
        for h in range(hp):
            acc = acc_ref[h]
            o_ref[:, h * MLA_V_DIM:(h + 1) * MLA_V_DIM] = (
                acc[:, 0:MLA_V_DIM] / acc[:, MLA_V_DIM:2 * MLA_V_DIM]).astype(o_ref.dtype)


def _flash_attention(q, k, v):
    bsz, heads, seq, dk = q.shape
    tq = min(FLASH_TQ, seq)
    tk = min(FLASH_TK, seq)
    hp = FLASH_HEADS
    qi, kj, fl = [], [], []
    for i in range(seq // tq):
        n_kv = ((i + 1) * tq) // tk
        for j in range(n_kv):
            qi.append(i)
            kj.append(j)
            masked = (j + 1) * tk - 1 > i * tq
            fl.append((_FIRST if j == 0 else 0) | (_LAST if j == n_kv - 1 else 0) | (_MASKED if masked else 0))
    qi = jnp.asarray(qi, jnp.int32)
    kj = jnp.asarray(kj, jnp.int32)
    fl = jnp.asarray(fl, jnp.int32)
    grid_spec = pltpu.PrefetchScalarGridSpec(
        num_scalar_prefetch=3,
        grid=(bsz, heads // hp, int(qi.shape[0])),
        in_specs=[pl.BlockSpec((None, hp, tq, dk), lambda b, h, n, qi, kj, fl: (b, h, qi[n], 0)),
                  pl.BlockSpec((None, hp, tk, dk), lambda b, h, n, qi, kj, fl: (b, h, kj[n], 0)),
                  pl.BlockSpec((None, tk, hp * MLA_V_DIM), lambda b, h, n, qi, kj, fl: (b, kj[n], h))],
        out_specs=pl.BlockSpec((None, tq, hp * MLA_V_DIM), lambda b, h, n, qi, kj, fl: (b, qi[n], h)),
        scratch_shapes=[pltpu.VMEM((hp, tq, LANES), F32),
                        pltpu.VMEM((hp, tq, 2 * MLA_V_DIM), F32)],
    )
    return pl.pallas_call(
        functools.partial(_flash_kernel, tq=tq, tk=tk, hp=hp),
        out_shape=jax.ShapeDtypeStruct((bsz, seq, heads * MLA_V_DIM), BF16),
        grid_spec=grid_spec,
        compiler_params=_cparams(("parallel", "parallel", "arbitrary")),
        name="mla_flash",
    )(qi, kj, fl, q, k, v)


def kernel(x, mem, positions, w_in_a, lam_re, lam_im, log_dt, b_re, b_im, c_re, c_im, d_skip,
           w_glu, b_glu, w_in_b, q_norm_g, w_uq, w_dkv, kv_norm_g, w_kr, w_uk, w_uv, w_mem_kv,
           w_out, ln_g, ln_b):
    bsz, seq, d_model = x.shape
    n_ssm = w_in_a.shape[0]
    n_mla = w_in_b.shape[0]
    depth = n_ssm + n_mla
    main_w = w_glu.shape[-1]
    mem_w = d_model - main_w
    q_lora = q_norm_g.shape[-1]
    alpha = (2.0 * depth) ** 0.25
    tokens = bsz * seq
    assert main_w % mem_w == 0 and q_lora == mem_w

    cos, sin = _rope_tables(positions)
    mem2d = mem.reshape(bsz * mem.shape[1], d_model)
    seg_len = min(S5_TIME_TILE, seq) // SUBLANES

    k_shared = v_shared = None
    for layer in range(depth):
        kv_mem = _inproj(mem2d, w_mem_kv[layer].astype(BF16), "mem_kv").reshape(bsz, mem.shape[1], 2 * mem_w)
        x2d = x.reshape(tokens, d_model)
        if layer < n_ssm:
            i = layer
            h = _inproj(x2d, w_in_a[i].astype(BF16), "inproj_a").reshape(bsz, seq, -1)
            a_re, a_im, al_re, al_im, bbt_re, bbt_im = _s5_params(
                lam_re[i], lam_im[i], log_dt[i], b_re[i], b_im[i], seg_len)
            y = _s5_scan(h, a_re, a_im, al_re, al_im, bbt_re, bbt_im, c_re[i], c_im[i], d_skip[i], main_w)
            main = _glu(y.reshape(tokens, main_w), w_glu[i].astype(BF16),
                        b_glu[i].reshape(1, main_w)).reshape(bsz, seq, main_w)
            z_blk = 1
            mq_blk = 2 * main_w // mem_w
        else:
            j = layer - n_ssm
            w_b = jnp.concatenate([w_in_b[j][:, q_lora:q_lora + main_w], w_in_b[j][:, :q_lora],
                                   w_in_b[j][:, q_lora + main_w:]], axis=-1).astype(BF16)
            h = _inproj(x2d, w_b, "inproj_b").reshape(bsz, seq, -1)
            q = _q_proj(h, main_w // q_lora, q_norm_g[j], w_uq[j], cos, sin)
            main = _flash_attention(q, k_shared, v_shared)
            z_blk = 0
            mq_blk = (main_w + q_lora) // mem_w
        x = _layer_tail(main, h, z_blk, mq_blk, mq_blk + 1, kv_mem, w_out[layer], x,
                        ln_g[layer], ln_b[layer], alpha)
        if layer == n_ssm - 1:
            k_shared, v_shared = _kv_shared(x, w_dkv, kv_norm_g, w_kr, w_uk, w_uv, cos, sin)
    return x
```

```python
import functools
import math

import jax
import jax.numpy as jnp
from jax import lax
from jax.experimental import pallas as pl
from jax.experimental.pallas import tpu as pltpu

F32 = jnp.float32
BF16 = jnp.bfloat16

MEM_HEADS = 4
MEM_HEAD_DIM = 128
SSM_GROUP = 16
SSM_STATE = 64
MLA_V_DIM = 128
MLA_NOPE = 128
MLA_ROPE = 64
MLA_QK = MLA_NOPE + MLA_ROPE
ROPE_BASE = 10000.0
LN_EPS = 1e-5
RMS_EPS = 1e-6
NEG_INF = -1e30

LANES = 128
VMEM_LIMIT = 52 * 1024 * 1024
S5_BLOCK = 16
S5_GROUP_WIDTH = S5_BLOCK * SSM_GROUP
S5_PAIRS_PER_CHUNK = 8
S5_BLOCKS_PER_TILE = 256
S5_PARAM_GROUPS = 8
FLASH_TQ = 512
FLASH_TK = 512
FLASH_HEADS = 12


def _cparams(sem):
    return pltpu.CompilerParams(dimension_semantics=sem, vmem_limit_bytes=VMEM_LIMIT)


def _dot(a, b):
    return jnp.dot(a, b, preferred_element_type=F32)


def _sigmoid(x):
    return 1.0 / (1.0 + jnp.exp(-x))


def _rope_table_kernel(pos_ref, invf_ref, sign_ref, cos_ref, sin_ref):
    ang = pos_ref[...].astype(F32) * invf_ref[...]
    cos_ref[...] = jnp.cos(ang)
    sin_ref[...] = jnp.sin(ang) * sign_ref[...]


def _rope_tables(positions):
    bsz, seq = positions.shape
    tokens = bsz * seq
    bm = min(2048, tokens)
    inv_freq = ROPE_BASE ** (-jnp.arange(0, MLA_ROPE, 2, dtype=F32) / MLA_ROPE)
    invf = jnp.tile(inv_freq, 4).reshape(1, LANES)
    sign = jnp.tile(jnp.concatenate([-jnp.ones((MLA_ROPE // 2,), F32),
                                     jnp.ones((MLA_ROPE // 2,), F32)]), 2).reshape(1, LANES)
    cos, sin = pl.pallas_call(
        _rope_table_kernel,
        out_shape=(jax.ShapeDtypeStruct((tokens, LANES), F32),
                   jax.ShapeDtypeStruct((tokens, LANES), F32)),
        grid=(tokens // bm,),
        in_specs=[pl.BlockSpec((bm, 1), lambda i: (i, 0)),
                  pl.BlockSpec((1, LANES), lambda i: (0, 0)),
                  pl.BlockSpec((1, LANES), lambda i: (0, 0))],
        out_specs=(pl.BlockSpec((bm, LANES), lambda i: (i, 0)),
                   pl.BlockSpec((bm, LANES), lambda i: (i, 0))),
        compiler_params=_cparams(("parallel",)),
        name="rope_tables",
    )(positions.reshape(tokens, 1), invf, sign)
    return cos.reshape(bsz, seq, LANES), sin.reshape(bsz, seq, LANES)


def _inproj_kernel(x_ref, w_ref, o_ref, xb_ref):
    @pl.when(pl.program_id(1) == 0)
    def _():
        xb_ref[...] = x_ref[...].astype(BF16)

    o_ref[...] = _dot(xb_ref[...], w_ref[...]).astype(o_ref.dtype)


def _inproj(x2d, w_bf16, name):
    m, k = x2d.shape
    n = w_bf16.shape[1]
    bm = min(512, m)
    bn = min(1024, n)
    return pl.pallas_call(
        _inproj_kernel,
        out_shape=jax.ShapeDtypeStruct((m, n), BF16),
        grid=(m // bm, n // bn),
        in_specs=[pl.BlockSpec((bm, k), lambda i, j: (i, 0)),
                  pl.BlockSpec((k, bn), lambda i, j: (0, j))],
        out_specs=pl.BlockSpec((bm, bn), lambda i, j: (i, j)),
        scratch_shapes=[pltpu.VMEM((bm, k), BF16)],
        compiler_params=_cparams(("parallel", "arbitrary")),
        name=name,
    )(x2d, w_bf16)


def _s5_param_kernel(lr_ref, li_ref, ldt_ref, br_ref, bi_ref, cr_ref, ci_ref,
                     alre_ref, alim_ref, car_ref, cai_ref, fre_ref, fim_ref, k_ref, *, blk):
    lr = lr_ref[...]
    li = li_ref[...]
    dt = jnp.exp(ldt_ref[...])

    def a_pow(k):
        mag = jnp.exp(lr * dt * float(k))
        ang = li * dt * float(k)
        return mag * jnp.cos(ang), mag * jnp.sin(ang)

    a_re, a_im = a_pow(1)
    den = lr * lr + li * li
    n_re = a_re - 1.0
    n_im = a_im
    coef_re = (n_re * lr + n_im * li) / den
    coef_im = (n_im * lr - n_re * li) / den
    br = br_ref[...]
    bi = bi_ref[...]
    bbr = coef_re * br - coef_im * bi
    bbi = coef_re * bi + coef_im * br
    cr = cr_ref[...]
    ci = ci_ref[...]
    w = SSM_GROUP
    for k in range(blk + 1):
        pr, pi = a_pow(k)
        car_ref[:, k * w:(k + 1) * w, :] = cr * pr - ci * pi
        cai_ref[:, k * w:(k + 1) * w, :] = cr * pi + ci * pr
        if k < blk:
            s = blk - 1 - k
            fre_ref[:, s * w:(s + 1) * w, :] = pr * bbr - pi * bbi
            fim_ref[:, s * w:(s + 1) * w, :] = pr * bbi + pi * bbr
    al_re, al_im = a_pow(blk)
    alre_ref[...] = al_re
    alim_ref[...] = al_im
    dims = (((2,), (2,)), ((0,), (0,)))
    kr = lax.dot_general(car_ref[:, 0:blk * w, :], bbr, dims, precision=lax.Precision.HIGHEST,
                         preferred_element_type=F32)
    ki = lax.dot_general(cai_ref[:, 0:blk * w, :], bbi, dims, precision=lax.Precision.HIGHEST,
                         preferred_element_type=F32)
    k_ref[...] = kr - ki


def _s5_params(lam_re, lam_im, log_dt, b_re, b_im, c_re, c_im):
    g, p = lam_re.shape
    w = SSM_GROUP
    blk = S5_BLOCK
    gb = S5_PARAM_GROUPS

    def spec(rows, cols):
        return pl.BlockSpec((gb, rows, cols), lambda i: (i, 0, 0))

    return pl.pallas_call(
        functools.partial(_s5_param_kernel, blk=blk),
        out_shape=(jax.ShapeDtypeStruct((g, 1, p), F32), jax.ShapeDtypeStruct((g, 1, p), F32),
                   jax.ShapeDtypeStruct((g, (blk + 1) * w, p), F32),
                   jax.ShapeDtypeStruct((g, (blk + 1) * w, p), F32),
                   jax.ShapeDtypeStruct((g, blk * w, p), F32), jax.ShapeDtypeStruct((g, blk * w, p), F32),
                   jax.ShapeDtypeStruct((g, blk * w, w), F32)),
        grid=(g // gb,),
        in_specs=[spec(1, p), spec(1, p), spec(1, 1), spec(w, p), spec(w, p), spec(w, p), spec(w, p)],
        out_specs=(spec(1, p), spec(1, p), spec((blk + 1) * w, p), spec((blk + 1) * w, p),
                   spec(blk * w, p), spec(blk * w, p), spec(blk * w, w)),
        compiler_params=_cparams(("parallel",)),
        name="s5_params",
    )(lam_re.reshape(g, 1, p), lam_im.reshape(g, 1, p), log_dt.reshape(g, 1, 1),
      jnp.swapaxes(b_re, 1, 2), jnp.swapaxes(b_im, 1, 2), c_re, c_im)


def _s5_kernel(u_ref, fre_ref, fim_ref, m_ref, ere_ref, eim_ref, al_ref, d_ref,
               y_ref, g_ref, carry_ref, *, pairs, nb):
    @pl.when(pl.program_id(2) == 0)
    def _():
        carry_ref[...] = jnp.zeros_like(carry_ref)

    sw = 2 * SSM_STATE
    gw = S5_GROUP_WIDTH
    for q in range(pairs):
        up = u_ref[q]
        g_ref[0, :, q * sw:(q + 1) * sw] = _dot(up, fre_ref[q])
        g_ref[1, :, q * sw:(q + 1) * sw] = _dot(up, fim_ref[q])

    al_re = al_ref[0:1, :]
    al_im = al_ref[1:2, :]

    def step(b, carry):
        hr, hi = carry
        row = pl.ds(b, 1)
        gr = g_ref[0, row, :]
        gi = g_ref[1, row, :]
        g_ref[0, row, :] = hr
        g_ref[1, row, :] = hi
        return al_re * hr - al_im * hi + gr, al_re * hi + al_im * hr + gi

    hr, hi = lax.fori_loop(0, nb, step, (carry_ref[0:1, :], carry_ref[1:2, :]), unroll=8)
    carry_ref[0:1, :] = hr
    carry_ref[1:2, :] = hi

    for q in range(pairs):
        up = u_ref[q]
        h_re = g_ref[0, :, q * sw:(q + 1) * sw].astype(BF16)
        h_im = g_ref[1, :, q * sw:(q + 1) * sw].astype(BF16)
        y_state = _dot(h_re, ere_ref[q]) - _dot(h_im, eim_ref[q])
        y_intra = jnp.concatenate([_dot(up[:, 0:gw], m_ref[2 * q]),
                                   _dot(up[:, gw:2 * gw], m_ref[2 * q + 1])], axis=1)
        y_ref[q] = y_state + y_intra + d_ref[q] * up.astype(F32)


def _s5_layer(h0, lam_re, lam_im, log_dt, b_re, b_im, c_re, c_im, d_skip, main_width):
    bsz, seq, _ = h0.shape
    groups, p = lam_re.shape
    w = SSM_GROUP
    blk = S5_BLOCK
    gw = S5_GROUP_WIDTH
    n_pairs = groups // 2
    pairs = S5_PAIRS_PER_CHUNK
    n_chunks = n_pairs // pairs
    n_blocks = seq // blk
    nb = min(S5_BLOCKS_PER_TILE, n_blocks)

    al_re, al_im, car, cai, fre, fim, kmat = _s5_params(lam_re, lam_im, log_dt, b_re, b_im, c_re, c_im)

    eye2 = jnp.eye(2, dtype=F32)

    def pair_rows(f):
        return jnp.einsum('qgnp,gk->qgnkp', f.reshape(n_pairs, 2, gw, p), eye2).reshape(
            n_pairs, 2 * gw, 2 * p).astype(BF16)

    def pair_cols(e):
        et = jnp.swapaxes(e, 1, 2).reshape(n_pairs, 2, p, gw)
        return jnp.einsum('qgpn,gk->qgpkn', et, eye2).reshape(n_pairs, 2 * p, 2 * gw).astype(BF16)

    fre_bd, fim_bd = pair_rows(fre), pair_rows(fim)
    ere_bd, eim_bd = pair_cols(car[:, w:, :]), pair_cols(cai[:, w:, :])
    k4 = kmat.reshape(groups, blk, w, w)
    s_idx = jnp.arange(blk)[:, None]
    t_idx = jnp.arange(blk)[None, :]
    tau = jnp.clip(t_idx - s_idx, 0, blk - 1)
    m5 = jnp.where((t_idx >= s_idx)[None, :, :, None, None], k4[:, tau], 0.0)
    m_t = m5.transpose(0, 1, 4, 2, 3).reshape(groups, gw, gw).astype(BF16)
    al = jnp.concatenate([al_re.reshape(n_chunks, 1, pairs * 2 * p),
                          al_im.reshape(n_chunks, 1, pairs * 2 * p)], axis=1)
    dvec = jnp.broadcast_to(d_skip.reshape(n_pairs, 2, 1, w), (n_pairs, 2, blk, w)).reshape(n_pairs, 1, 2 * gw)

    u = h0[..., :main_width].reshape(bsz, n_blocks, blk, n_pairs, 2, w)
    ug = u.transpose(0, 3, 1, 4, 2, 5).reshape(bsz, n_pairs, n_blocks, 2 * gw)

    yg = pl.pallas_call(
        functools.partial(_s5_kernel, pairs=pairs, nb=nb),
        out_shape=jax.ShapeDtypeStruct((bsz, n_pairs, n_blocks, 2 * gw), F32),
        grid=(bsz, n_chunks, n_blocks // nb),
        in_specs=[pl.BlockSpec((None, pairs, nb, 2 * gw), lambda b, c, t: (b, c, t, 0)),
                  pl.BlockSpec((pairs, 2 * gw, 2 * p), lambda b, c, t: (c, 0, 0)),
                  pl.BlockSpec((pairs, 2 * gw, 2 * p), lambda b, c, t: (c, 0, 0)),
                  pl.BlockSpec((2 * pairs, gw, gw), lambda b, c, t: (c, 0, 0)),
                  pl.BlockSpec((pairs, 2 * p, 2 * gw), lambda b, c, t: (c, 0, 0)),
                  pl.BlockSpec((pairs, 2 * p, 2 * gw), lambda b, c, t: (c, 0, 0)),
                  pl.BlockSpec((None, 2, pairs * 2 * p), lambda b, c, t: (c, 0, 0)),
                  pl.BlockSpec((pairs, 1, 2 * gw), lambda b, c, t: (c, 0, 0))],
        out_specs=pl.BlockSpec((None, pairs, nb, 2 * gw), lambda b, c, t: (b, c, t, 0)),
        scratch_shapes=[pltpu.VMEM((2, nb, pairs * 2 * p), F32),
                        pltpu.VMEM((2, pairs * 2 * p), F32)],
        compiler_params=_cparams(("parallel", "parallel", "arbitrary")),
        name="s5_blocks",
    )(ug, fre_bd, fim_bd, m_t, ere_bd, eim_bd, al, dvec)

    y = yg.reshape(bsz, n_pairs, n_blocks, 2, blk, w).transpose(0, 2, 4, 1, 3, 5)
    return y.reshape(bsz, seq, main_width)


def _glu_kernel(y_ref, w_ref, b_ref, o_ref):
    y = y_ref[...]
    g = y * (0.5 * (1.0 + jnp.tanh(math.sqrt(2.0 / math.pi) * (y + 0.044715 * (y * y * y)))))
    s = _dot(g.astype(BF16), w_ref[...]) + b_ref[...]
    o_ref[...] = (g * _sigmoid(s)).astype(o_ref.dtype)


def _glu(y2d, w_bf16, b_row):
    m, n = y2d.shape
    bm = min(512, m)
    return pl.pallas_call(
        _glu_kernel,
        out_shape=jax.ShapeDtypeStruct((m, n), BF16),
        grid=(m // bm,),
        in_specs=[pl.BlockSpec((bm, n), lambda i: (i, 0)),
                  pl.BlockSpec((n, n), lambda i: (0, 0)),
                  pl.BlockSpec((1, n), lambda i: (0, 0))],
        out_specs=pl.BlockSpec((bm, n), lambda i: (i, 0)),
        compiler_params=_cparams(("parallel",)),
        name="s5_glu",
    )(y2d, w_bf16, b_row)


def _tail_kernel(main_ref, z_ref, mq_ref, mz_ref, kv_ref, wm_ref, wmem_ref, x_ref, g_ref, b_ref,
                 o_ref, *, alpha):
    z = z_ref[...].astype(F32)
    a_main = (main_ref[...].astype(F32) * (z * _sigmoid(z))).astype(BF16)

    mq = mq_ref[...]
    kv = kv_ref[...]
    mem_w = MEM_HEADS * MEM_HEAD_DIM
    outs = []
    for h in range(MEM_HEADS):
        q = mq[:, h * MEM_HEAD_DIM:(h + 1) * MEM_HEAD_DIM]
        k = kv[:, h * MEM_HEAD_DIM:(h + 1) * MEM_HEAD_DIM]
        v = kv[:, mem_w + h * MEM_HEAD_DIM:mem_w + (h + 1) * MEM_HEAD_DIM]
        s = lax.dot_general(q, k, (((1,), (1,)), ((), ())),
                            preferred_element_type=F32) * (MEM_HEAD_DIM ** -0.5)
        p = jnp.exp(s - jnp.max(s, axis=-1, keepdims=True))
        l = jnp.sum(p, axis=-1, keepdims=True)
        outs.append(_dot(p.astype(BF16), v) / l)
    mz = mz_ref[...].astype(F32)
    memo = (jnp.concatenate(outs, axis=-1) * (mz * _sigmoid(mz))).astype(BF16)

    y = _dot(a_main, wm_ref[...]) + _dot(memo, wmem_ref[...])
    r = alpha * x_ref[...] + y
    mu = jnp.mean(r, axis=-1, keepdims=True)
    d = r - mu
    var = jnp.mean(d * d, axis=-1, keepdims=True)
    o_ref[...] = d * lax.rsqrt(var + LN_EPS) * g_ref[...] + b_ref[...]


def _layer_tail(main, h, z_blk, mq_blk, mz_blk, kv_mem, w_out, x, ln_g, ln_b, alpha):
    bsz, seq, d_model = x.shape
    main_w = main.shape[-1]
    mem_w = d_model - main_w
    bm = min(256, seq)
    w_main = w_out[:main_w].astype(BF16)
    w_mem = w_out[main_w:].astype(BF16)
    return pl.pallas_call(
        functools.partial(_tail_kernel, alpha=alpha),
        out_shape=jax.ShapeDtypeStruct((bsz, seq, d_model), F32),
        grid=(bsz, seq // bm),
        in_specs=[pl.BlockSpec((None, bm, main_w), lambda b, i: (b, i, 0)),
                  pl.BlockSpec((None, bm, main_w), lambda b, i: (b, i, z_blk)),
                  pl.BlockSpec((None, bm, mem_w), lambda b, i: (b, i, mq_blk)),
                  pl.BlockSpec((None, bm, mem_w), lambda b, i: (b, i, mz_blk)),
                  pl.BlockSpec((None,) + kv_mem.shape[1:], lambda b, i: (b, 0, 0)),
                  pl.BlockSpec((main_w, d_model), lambda b, i: (0, 0)),
                  pl.BlockSpec((mem_w, d_model), lambda b, i: (0, 0)),
                  pl.BlockSpec((None, bm, d_model), lambda b, i: (b, i, 0)),
                  pl.BlockSpec((1, d_model), lambda b, i: (0, 0)),
                  pl.BlockSpec((1, d_model), lambda b, i: (0, 0))],
        out_specs=pl.BlockSpec((None, bm, d_model), lambda b, i: (b, i, 0)),
        compiler_params=_cparams(("parallel", "parallel")),
        name="layer_tail",
    )(main, h, h, h, kv_mem, w_main, w_mem, x, ln_g.reshape(1, d_model), ln_b.reshape(1, d_model))


def _rms(c, g):
    return c * lax.rsqrt(jnp.mean(c * c, axis=-1, keepdims=True) + RMS_EPS) * g


def _rope_rotate(pair, cos, sin):
    return pair * cos + pltpu.roll(pair, MLA_ROPE, 1) * sin


def _kv_kernel(x_ref, wd_ref, g_ref, wkr_ref, wukv_ref, cos_ref, sin_ref, k_ref, v_ref, *, heads):
    xb = x_ref[...].astype(BF16)
    c = _rms(_dot(xb, wd_ref[...]), g_ref[...]).astype(BF16)
    krot = _rope_rotate(_dot(xb, wkr_ref[...]), cos_ref[...], sin_ref[...])[:, 0:MLA_ROPE].astype(BF16)
    kv = _dot(c, wukv_ref[...])
    wh = MLA_NOPE + MLA_V_DIM
    for h in range(heads):
        k_ref[h, :, 0:MLA_NOPE] = kv[:, h * wh:h * wh + MLA_NOPE].astype(BF16)
        k_ref[h, :, MLA_NOPE:MLA_QK] = krot
        v_ref[:, h * MLA_V_DIM:(h + 1) * MLA_V_DIM] = kv[:, h * wh + MLA_NOPE:(h + 1) * wh].astype(BF16)


def _swap_halves(w):
    half = w.shape[-1] // 2
    return jnp.concatenate([w[..., half:], w[..., :half]], axis=-1)


def _kv_shared(x, w_dkv, kv_norm_g, w_kr, w_uk, w_uv, cos, sin):
    bsz, seq, d_model = x.shape
    lora, heads, _ = w_uk.shape
    bm = min(512, seq)
    w_kr2 = jnp.concatenate([w_kr, _swap_halves(w_kr)], axis=-1).astype(BF16)
    w_ukv = jnp.concatenate([w_uk, w_uv], axis=-1).reshape(lora, heads * (MLA_NOPE + MLA_V_DIM)).astype(BF16)
    return pl.pallas_call(
        functools.partial(_kv_kernel, heads=heads),
        out_shape=(jax.ShapeDtypeStruct((bsz, heads, seq, MLA_QK), BF16),
                   jax.ShapeDtypeStruct((bsz, seq, heads * MLA_V_DIM), BF16)),
        grid=(bsz, seq // bm),
        in_specs=[pl.BlockSpec((None, bm, d_model), lambda b, i: (b, i, 0)),
                  pl.BlockSpec((d_model, lora), lambda b, i: (0, 0)),
                  pl.BlockSpec((1, lora), lambda b, i: (0, 0)),
                  pl.BlockSpec((d_model, LANES), lambda b, i: (0, 0)),
                  pl.BlockSpec(w_ukv.shape, lambda b, i: (0, 0)),
                  pl.BlockSpec((None, bm, LANES), lambda b, i: (b, i, 0)),
                  pl.BlockSpec((None, bm, LANES), lambda b, i: (b, i, 0))],
        out_specs=(pl.BlockSpec((None, heads, bm, MLA_QK), lambda b, i: (b, 0, i, 0)),
                   pl.BlockSpec((None, bm, heads * MLA_V_DIM), lambda b, i: (b, i, 0))),
        compiler_params=_cparams(("parallel", "parallel")),
        name="mla_kv",
    )(x, w_dkv.astype(BF16), kv_norm_g.reshape(1, lora), w_kr2, w_ukv, cos, sin)


def _q_kernel(cq_ref, g_ref, w_ref, cos_ref, sin_ref, q_ref, *, heads, scale):
    c = _rms(cq_ref[...].astype(F32), g_ref[...]).astype(BF16)
    qa = _dot(c, w_ref[...])
    cos = cos_ref[...]
    sin = sin_ref[...]
    wh = MLA_NOPE + 2 * MLA_ROPE
    for h in range(heads):
        q_ref[h, :, 0:MLA_NOPE] = (qa[:, h * wh:h * wh + MLA_NOPE] * scale).astype(BF16)
        rot = _rope_rotate(qa[:, h * wh + MLA_NOPE:(h + 1) * wh], cos, sin)
        q_ref[h, :, MLA_NOPE:MLA_QK] = (rot[:, 0:MLA_ROPE] * scale).astype(BF16)


def _q_proj(h, cq_blk, q_norm_g, w_uq, cos, sin):
    bsz, seq, _ = h.shape
    lora, heads, _ = w_uq.shape
    bm = min(512, seq)
    rope_w = w_uq[..., MLA_NOPE:]
    w2 = jnp.concatenate([w_uq, _swap_halves(rope_w)], axis=-1)
    w2 = w2.reshape(lora, heads * (MLA_NOPE + 2 * MLA_ROPE)).astype(BF16)
    return pl.pallas_call(
        functools.partial(_q_kernel, heads=heads, scale=MLA_QK ** -0.5 * math.log2(math.e)),
        out_shape=jax.ShapeDtypeStruct((bsz, heads, seq, MLA_QK), BF16),
        grid=(bsz, seq // bm),
        in_specs=[pl.BlockSpec((None, bm, lora), lambda b, i: (b, i, cq_blk)),
                  pl.BlockSpec((1, lora), lambda b, i: (0, 0)),
                  pl.BlockSpec(w2.shape, lambda b, i: (0, 0)),
                  pl.BlockSpec((None, bm, LANES), lambda b, i: (b, i, 0)),
                  pl.BlockSpec((None, bm, LANES), lambda b, i: (b, i, 0))],
        out_specs=pl.BlockSpec((None, heads, bm, MLA_QK), lambda b, i: (b, 0, i, 0)),
        compiler_params=_cparams(("parallel", "parallel")),
        name="mla_q",
    )(h, q_norm_g.reshape(1, lora), w2, cos, sin)


_FIRST, _LAST, _MASKED = 1, 2, 4


def _flash_kernel(qi_ref, kj_ref, fl_ref, q_ref, k_ref, v_ref, o_ref, m_ref, acc_ref, *, tq, tk, hp):
    n = pl.program_id(2)
    flags = fl_ref[n]

    @pl.when((flags & _FIRST) != 0)
    def _():
        m_ref[...] = jnp.full(m_ref.shape, NEG_INF, F32)
        acc_ref[...] = jnp.zeros_like(acc_ref)

    def update(masked):
        if masked:
            row = qi_ref[n] * tq + lax.broadcasted_iota(jnp.int32, (tq, tk), 0)
            col = kj_ref[n] * tk + lax.broadcasted_iota(jnp.int32, (tq, tk), 1)
            keep = col <= row
        ones = jnp.ones((tk, LANES), BF16)
        for h in range(hp):
            s = lax.dot_general(q_ref[h], k_ref[h], (((1,), (1,)), ((), ())),
                                preferred_element_type=F32)
            if masked:
                s = jnp.where(keep, s, NEG_INF)
            m_prev = m_ref[h]
            m_new = jnp.maximum(m_prev, jnp.max(s, axis=-1, keepdims=True))
            alpha = jnp.exp2(m_prev - m_new)
            p = jnp.exp2(s - jnp.concatenate([m_new] * (tk // LANES), axis=1)).astype(BF16)
            v1 = jnp.concatenate([v_ref[:, h * MLA_V_DIM:(h + 1) * MLA_V_DIM], ones], axis=1)
            acc_ref[h] = jnp.concatenate([alpha, alpha], axis=1) * acc_ref[h] + _dot(p, v1)
            m_ref[h] = m_new

    @pl.when((flags & _MASKED) == 0)
    def _():
        update(False)

    @pl.when((flags & _MASKED) != 0)
    def _():
        update(True)

    @pl.when((flags & _LAST) != 0)
    def _():
        for h in range(hp):
            acc = acc_ref[h]
            o_ref[:, h * MLA_V_DIM:(h + 1) * MLA_V_DIM] = (
                acc[:, 0:MLA_V_DIM] / acc[:, MLA_V_DIM:2 * MLA_V_DIM]).astype(o_ref.dtype)


def _flash_attention(q, k, v):
    bsz, heads, seq, dk = q.shape
    tq = min(FLASH_TQ, seq)
    tk = min(FLASH_TK, seq)
    hp = FLASH_HEADS
    qi, kj, fl = [], [], []
    for i in range(seq // tq):
        n_kv = ((i + 1) * tq) // tk
        for j in range(n_kv):
            qi.append(i)
            kj.append(j)
            masked = (j + 1) * tk - 1 > i * tq
            fl.append((_FIRST if j == 0 else 0) | (_LAST if j == n_kv - 1 else 0) | (_MASKED if masked else 0))
    qi = jnp.asarray(qi, jnp.int32)
    kj = jnp.asarray(kj, jnp.int32)
    fl = jnp.asarray(fl, jnp.int32)
    grid_spec = pltpu.PrefetchScalarGridSpec(
        num_scalar_prefetch=3,
        grid=(bsz, heads // hp, int(qi.shape[0])),
        in_specs=[pl.BlockSpec((None, hp, tq, dk), lambda b, h, n, qi, kj, fl: (b, h, qi[n], 0)),
                  pl.BlockSpec((None, hp, tk, dk), lambda b, h, n, qi, kj, fl: (b, h, kj[n], 0)),
                  pl.BlockSpec((None, tk, hp * MLA_V_DIM), lambda b, h, n, qi, kj, fl: (b, kj[n], h))],
        out_specs=pl.BlockSpec((None, tq, hp * MLA_V_DIM), lambda b, h, n, qi, kj, fl: (b, qi[n], h)),
        scratch_shapes=[pltpu.VMEM((hp, tq, LANES), F32),
                        pltpu.VMEM((hp, tq, 2 * MLA_V_DIM), F32)],
    )
    return pl.pallas_call(
        functools.partial(_flash_kernel, tq=tq, tk=tk, hp=hp),
        out_shape=jax.ShapeDtypeStruct((bsz, seq, heads * MLA_V_DIM), BF16),
        grid_spec=grid_spec,
        compiler_params=_cparams(("parallel", "parallel", "arbitrary")),
        name="mla_flash",
    )(qi, kj, fl, q, k, v)


def kernel(x, mem, positions, w_in_a, lam_re, lam_im, log_dt, b_re, b_im, c_re, c_im, d_skip,
           w_glu, b_glu, w_in_b, q_norm_g, w_uq, w_dkv, kv_norm_g, w_kr, w_uk, w_uv, w_mem_kv,
           w_out, ln_g, ln_b):
    bsz, seq, d_model = x.shape
    n_ssm = w_in_a.shape[0]
    n_mla = w_in_b.shape[0]
    depth = n_ssm + n_mla
    main_w = w_glu.shape[-1]
    mem_w = d_model - main_w
    q_lora = q_norm_g.shape[-1]
    alpha = (2.0 * depth) ** 0.25
    tokens = bsz * seq
    assert main_w % mem_w == 0 and q_lora == mem_w

    cos, sin = _rope_tables(positions)
    mem2d = mem.reshape(bsz * mem.shape[1], d_model)

    k_shared = v_shared = None
    for layer in range(depth):
        kv_mem = _inproj(mem2d, w_mem_kv[layer].astype(BF16), "mem_kv").reshape(bsz, mem.shape[1], 2 * mem_w)
        x2d = x.reshape(tokens, d_model)
        if layer < n_ssm:
            i = layer
            h = _inproj(x2d, w_in_a[i].astype(BF16), "inproj_a").reshape(bsz, seq, -1)
            y = _s5_layer(h, lam_re[i], lam_im[i], log_dt[i], b_re[i], b_im[i], c_re[i], c_im[i],
                          d_skip[i], main_w)
            main = _glu(y.reshape(tokens, main_w), w_glu[i].astype(BF16),
                        b_glu[i].reshape(1, main_w)).reshape(bsz, seq, main_w)
            z_blk = 1
            mq_blk = 2 * main_w // mem_w
        else:
            j = layer - n_ssm
            w_b = jnp.concatenate([w_in_b[j][:, q_lora:q_lora + main_w], w_in_b[j][:, :q_lora],
                                   w_in_b[j][:, q_lora + main_w:]], axis=-1).astype(BF16)
            h = _inproj(x2d, w_b, "inproj_b").reshape(bsz, seq, -1)
            q = _q_proj(h, main_w // q_lora, q_norm_g[j], w_uq[j], cos, sin)
            main = _flash_attention(q, k_shared, v_shared)
            z_blk = 0
            mq_blk = (main_w + q_lora) // mem_w
        x = _layer_tail(main, h, z_blk, mq_blk, mq_blk + 1, kv_mem, w_out[layer], x,
                        ln_g[layer], ln_b[layer], alpha)
        if layer == n_ssm - 1:
            k_shared, v_shared = _kv_shared(x, w_dkv, kv_norm_g, w_kr, w_uk, w_uv, cos, sin)
    return x
```
